```python
import math
import jax, jax.numpy as jnp
from jax import lax
import numpy as np


D_MODEL = 1024
BATCH = 32
SEQ = 2048
DEPTH = 4
DEC_BATCH = 32
DEC_SEQ = 64
PAST_LEN = 4096

CHUNK = 64
D_MIX = D_MODEL
A_HEADS = 4
A_HEAD_DIM = 64
A_WIDTH = A_HEADS * 2 * A_HEAD_DIM
ROT_DIM = A_HEAD_DIM // 4
ROPE_THETA = 500000.0
B_HEADS = 4
B_HEAD_DIM = 64
B_WIDTH = B_HEADS * B_HEAD_DIM
GMLP_CHUNK = 128
C_WIDTH = D_MIX - A_WIDTH - B_WIDTH
CONV_WIDTH = 31
D_IN = 3 * A_WIDTH + 2 * B_WIDTH + 2 * C_WIDTH
D_FF = 4 * D_MODEL
Q_BLOCK = 128
EPS = 1e-6
NEG_INF = -1e30

kernel_name = 'hymba_diffattn_gmlp_conformer_stream_step'


def rms_norm(x, g):
    xf = x.astype(jnp.float32)
    y = xf * lax.rsqrt(jnp.mean(xf * xf, axis=-1, keepdims=True) + EPS)
    return (y * g.astype(jnp.float32)).astype(x.dtype)


def layer_norm(x, g, b):
    xf = x.astype(jnp.float32)
    mu = jnp.mean(xf, axis=-1, keepdims=True)
    var = jnp.mean(jnp.square(xf - mu), axis=-1, keepdims=True)
    y = (xf - mu) * lax.rsqrt(var + EPS)
    return (y * g.astype(jnp.float32) + b.astype(jnp.float32)).astype(x.dtype)


def rope(x, pos):
    half = ROT_DIM // 2
    freqs = ROPE_THETA ** (-jnp.arange(0, ROT_DIM, 2, dtype=jnp.float32) / ROT_DIM)
    ang = pos[:, None] * freqs[None, :]
    cos = jnp.cos(ang)[None, :, None, :].astype(x.dtype)
    sin = jnp.sin(ang)[None, :, None, :].astype(x.dtype)
    x1 = x[..., :half]
    x2 = x[..., half:ROT_DIM]
    return jnp.concatenate([x1 * cos - x2 * sin, x2 * cos + x1 * sin, x[..., ROT_DIM:]], axis=-1)


def diff_lambda(lam_qk, layer_idx):
    lam_init = 0.8 - 0.6 * math.exp(-0.3 * layer_idx)
    lq = lam_qk.astype(jnp.float32)
    lam = jnp.exp(jnp.sum(lq[0] * lq[1])) - jnp.exp(jnp.sum(lq[2] * lq[3])) + lam_init
    return lam, lam_init


def diff_attend(q, k, v, lam, mask):
    bsz, nq = q.shape[0], q.shape[1]
    nk = k.shape[1]
    s = jnp.einsum('bqhd,bkhd->bhqk', q, k, preferred_element_type=jnp.float32) * (A_HEAD_DIM ** -0.5)
    if mask is not None:
        s = jnp.where(mask[None, None], s, NEG_INF)
    p = jax.nn.softmax(s, axis=-1).reshape(bsz, A_HEADS, 2, nq, nk)
    a = p[:, :, 0] - lam * p[:, :, 1]
    return jnp.einsum('bhqk,bkhe->bqhe', a.astype(v.dtype), v)


def diff_attn_prompt(q, k, v, lam):
    bsz, s_len = q.shape[0], q.shape[1]
    nb = s_len // Q_BLOCK
    qb = q.reshape(bsz, nb, Q_BLOCK, 2 * A_HEADS, A_HEAD_DIM).transpose(1, 0, 2, 3, 4)
    key_chunk = jnp.arange(s_len) // CHUNK

    def block(args):
        qi, i = args
        q_chunk = (i * Q_BLOCK + jnp.arange(Q_BLOCK)) // CHUNK
        mask = key_chunk[None, :] <= q_chunk[:, None]
        return diff_attend(qi, k, v, lam, mask)

    out = lax.map(block, (qb, jnp.arange(nb)))
    return out.transpose(1, 0, 2, 3, 4).reshape(bsz, s_len, A_HEADS, 2 * A_HEAD_DIM)


def gmlp_spatial_gate(zb, g_norm, w_s, bias):
    bsz, t = zb.shape[0], zb.shape[1]
    L = min(t, GMLP_CHUNK)
    nc = t // L
    u = zb[..., :B_WIDTH]
    vn = rms_norm(zb[..., B_WIDTH:], g_norm).reshape(bsz, t, B_HEADS, B_HEAD_DIM)
    tri = jnp.tril(jnp.ones((L, L), dtype=w_s.dtype))
    w = w_s[:, :L, :L] * tri[None]
    mixed = jnp.einsum('hts,bcshd->bcthd', w, vn.reshape(bsz, nc, L, B_HEADS, B_HEAD_DIM))
    mixed = mixed + bias[:, :L].T[None, None, :, :, None]
    return u * mixed.reshape(bsz, t, B_WIDTH), vn


def conv_module(conv_in, w, b, ln_g, ln_b):
    y = lax.conv_general_dilated(conv_in, w.reshape(CONV_WIDTH, 1, C_WIDTH), window_strides=(1,),
                                 padding='VALID', dimension_numbers=('NWC', 'WIO', 'NWC'),
                                 feature_group_count=C_WIDTH) + b
    y = layer_norm(y, ln_g, ln_b)
    return jax.nn.silu(y)


def encoder_layer(x, pos0, layer_idx, cache_k, cache_v, cache_conv, lw):
    (g_mix_pre, g_mix_post, g_mlp_pre, g_mlp_post, w_in, lam_qk, subln, gmlp_norm, gmlp_w_s,
     gmlp_bias, conv_w, conv_b, ln_g, ln_b, w_out, w_up, w_down) = lw
    bsz, t = x.shape[0], x.shape[1]
    h = rms_norm(x, g_mix_pre)
    z = jnp.einsum('btd,de->bte', h, w_in)
    q = z[..., :A_WIDTH].reshape(bsz, t, 2 * A_HEADS, A_HEAD_DIM)
    k = z[..., A_WIDTH:2 * A_WIDTH].reshape(bsz, t, 2 * A_HEADS, A_HEAD_DIM)
    v = z[..., 2 * A_WIDTH:3 * A_WIDTH].reshape(bsz, t, A_HEADS, 2 * A_HEAD_DIM)
    zb = jax.nn.gelu(z[..., 3 * A_WIDTH:3 * A_WIDTH + 2 * B_WIDTH])
    zc = z[..., 3 * A_WIDTH + 2 * B_WIDTH:]
    pos = (pos0 + jnp.arange(t)).astype(jnp.float32)
    q = rope(q, pos)
    k = rope(k, pos)
    lam, lam_init = diff_lambda(lam_qk, layer_idx)
    glu = zc[..., :C_WIDTH] * jax.nn.sigmoid(zc[..., C_WIDTH:])
    if cache_k is None:
        a = diff_attn_prompt(q, k, v, lam)
        conv_in = jnp.pad(glu, ((0, 0), (CONV_WIDTH - 1, 0), (0, 0)))
    else:
        a = diff_attend(q, jnp.concatenate([cache_k, k], axis=1),
                        jnp.concatenate([cache_v, v], axis=1), lam, None)
        conv_in = jnp.concatenate([cache_conv, glu], axis=1)
    new_conv = conv_in[:, conv_in.shape[1] - (CONV_WIDTH - 1):]
    a = (rms_norm(a, subln) * (1.0 - lam_init)).reshape(bsz, t, A_WIDTH)
    b_out, v_rows = gmlp_spatial_gate(zb, gmlp_norm, gmlp_w_s, gmlp_bias)
    c_out = conv_module(conv_in, conv_w, conv_b, ln_g, ln_b)
    mix = jnp.einsum('bte,ed->btd', jnp.concatenate([a, b_out, c_out], axis=-1), w_out)
    x = x + rms_norm(mix, g_mix_post)
    h = rms_norm(x, g_mlp_pre)
    f = jnp.einsum('btf,fd->btd', jnp.square(jax.nn.relu(jnp.einsum('btd,df->btf', h, w_up))), w_down)
    x = x + rms_norm(f, g_mlp_post)
    return x, k, v, new_conv, v_rows


def setup_inputs(seed: int = 0) -> dict:
    key = jax.random.key(seed)
    ks = jax.random.split(key, 24)

    def nrm(k, shape, scale):
        return scale * jax.random.normal(k, shape, jnp.float32)

    return {
        'x_prompt': nrm(ks[0], (BATCH, SEQ, D_MODEL), 1.0),
        'x_sample': nrm(ks[1], (DEC_BATCH, DEC_SEQ, D_MODEL), 1.0),
        'cache_k': nrm(ks[2], (DEPTH, DEC_BATCH, PAST_LEN, 2 * A_HEADS, A_HEAD_DIM), 1.0),
        'cache_v': nrm(ks[3], (DEPTH, DEC_BATCH, PAST_LEN, A_HEADS, 2 * A_HEAD_DIM), 1.0),
        'cache_conv': nrm(ks[4], (DEPTH, DEC_BATCH, CONV_WIDTH - 1, C_WIDTH), 0.5),
        'norm_mix_pre': 1.0 + nrm(ks[5], (DEPTH, D_MODEL), 0.05),
        'norm_mix_post': 1.0 + nrm(ks[6], (DEPTH, D_MODEL), 0.05),
        'norm_mlp_pre': 1.0 + nrm(ks[7], (DEPTH, D_MODEL), 0.05),
        'norm_mlp_post': 1.0 + nrm(ks[8], (DEPTH, D_MODEL), 0.05),
        'w_in': nrm(ks[9], (DEPTH, D_MODEL, D_IN), D_MODEL ** -0.5),
        'diff_lambda': nrm(ks[10], (DEPTH, 4, A_HEAD_DIM), 0.1),
        'diff_subln': 1.0 + nrm(ks[11], (DEPTH, 2 * A_HEAD_DIM), 0.05),
        'gmlp_norm': 1.0 + nrm(ks[12], (DEPTH, B_WIDTH), 0.05),
        'gmlp_w_s': nrm(ks[13], (DEPTH, B_HEADS, GMLP_CHUNK, GMLP_CHUNK), GMLP_CHUNK ** -0.5),
        'gmlp_bias': 1.0 + nrm(ks[14], (DEPTH, B_HEADS, GMLP_CHUNK), 0.1),
        'conv_w': nrm(ks[15], (DEPTH, CONV_WIDTH, C_WIDTH), CONV_WIDTH ** -0.5),
        'conv_b': nrm(ks[16], (DEPTH, C_WIDTH), 0.01),
        'conv_ln_gain': 1.0 + nrm(ks[17], (DEPTH, C_WIDTH), 0.05),
        'conv_ln_bias': nrm(ks[18], (DEPTH, C_WIDTH), 0.01),
        'w_out': nrm(ks[19], (DEPTH, D_MIX, D_MODEL), D_MIX ** -0.5),
        'w_up': nrm(ks[20], (DEPTH, D_MODEL, D_FF), D_MODEL ** -0.5),
        'w_down': nrm(ks[21], (DEPTH, D_FF, D_MODEL), D_FF ** -0.5),
    }


def reference(x_prompt, x_sample, cache_k, cache_v, cache_conv, norm_mix_pre, norm_mix_post,
              norm_mlp_pre, norm_mlp_post, w_in, diff_lambda, diff_subln, gmlp_norm, gmlp_w_s,
              gmlp_bias, conv_w, conv_b, conv_ln_gain, conv_ln_bias, w_out, w_up, w_down):
    xp = x_prompt
    xs = x_sample
    kp, vp, cp, ksl, vsl, csl, gsl = [], [], [], [], [], [], []
    for l in range(DEPTH):
        lw = (norm_mix_pre[l], norm_mix_post[l], norm_mlp_pre[l], norm_mlp_post[l], w_in[l],
              diff_lambda[l], diff_subln[l], gmlp_norm[l], gmlp_w_s[l], gmlp_bias[l], conv_w[l],
              conv_b[l], conv_ln_gain[l], conv_ln_bias[l], w_out[l], w_up[l], w_down[l])
        xp, k_p, v_p, c_p, _ = encoder_layer(xp, 0, l, None, None, None, lw)
        xs, k_s, v_s, c_s, g_s = encoder_layer(xs, PAST_LEN, l, cache_k[l], cache_v[l], cache_conv[l], lw)
        kp.append(k_p)
        vp.append(v_p)
        cp.append(c_p)
        ksl.append(k_s)
        vsl.append(v_s)
        csl.append(c_s)
        gsl.append(g_s)
    new_k_prompt = jnp.stack(kp)
    new_v_prompt = jnp.stack(vp)
    new_conv_prompt = jnp.stack(cp)
    new_k_sample = jnp.stack(ksl)
    new_v_sample = jnp.stack(vsl)
    new_conv_sample = jnp.stack(csl)
    new_gmlp_v_sample = jnp.stack(gsl)
    return (xp, xs, new_k_prompt, new_v_prompt, new_conv_prompt, new_k_sample, new_v_sample,
            new_conv_sample, new_gmlp_v_sample)
```

```python
import functools
import math

import jax
import jax.numpy as jnp
from jax import lax
from jax.experimental import pallas as pl
from jax.experimental.pallas import tpu as pltpu

D_MODEL = 1024
A_HEADS = 4
A_HEAD_DIM = 64
A_WIDTH = A_HEADS * 2 * A_HEAD_DIM
PAIR = 2 * A_HEAD_DIM
ROT_DIM = A_HEAD_DIM // 4
ROPE_THETA = 500000.0
B_HEADS = 4
B_HEAD_DIM = 64
B_WIDTH = B_HEADS * B_HEAD_DIM
GMLP_CHUNK = 128
C_WIDTH = 256
CONV_WIDTH = 31
HIST = CONV_WIDTH - 1
HIST_PAD = 32
D_IN = 3 * A_WIDTH + 2 * B_WIDTH + 2 * C_WIDTH
D_FF = 4 * D_MODEL
EPS = 1e-6
NEG_INF = -1e30
CHUNK = 64
CHUNK_SHIFT = CHUNK.bit_length() - 1

VMEM_LIMIT_BYTES = 56 * 1024 * 1024
FRONT_ROWS = 512
ATTN_TILE = 512
CACHE_TILE = 1024
BACK_ROWS = 512
FF_CHUNK = 1024

_BF16 = jnp.bfloat16
_F32 = jnp.float32


def _rms(x, g):
    return x * lax.rsqrt(jnp.mean(x * x, axis=-1, keepdims=True) + EPS) * g


def _gelu_tanh(x):
    c = math.sqrt(2.0 / math.pi)
    return 0.5 * x * (1.0 + jnp.tanh(c * (x + 0.044715 * (x * x * x))))


def _sigmoid(x):
    return 1.0 / (1.0 + jnp.exp(-x))


def _rope_pairs(z, c, s1, s2):
    outs = []
    for j in range(A_WIDTH // PAIR):
        zj = z[:, j * PAIR:(j + 1) * PAIR]
        up = pltpu.roll(zj, ROT_DIM // 2, axis=1)
        dn = pltpu.roll(zj, PAIR - ROT_DIM // 2, axis=1)
        outs.append(zj * c + up * s1 + dn * s2)
    return jnp.concatenate(outs, axis=1)


def _front_kernel(*refs, bb, tt, chunk, prompt):
    if prompt:
        (x_ref, c_ref, s1_ref, s2_ref, gpre_ref, win_ref, gn_ref, ws_ref, bias_ref, cw_ref,
         cb_ref, lg_ref, lb_ref,
         q_ref, k_ref, v_ref, kb_ref, vb_ref, bc_ref, nc_ref, buf_ref) = refs
    else:
        (x_ref, c_ref, s1_ref, s2_ref, gpre_ref, win_ref, gn_ref, ws_ref, bias_ref, cw_ref,
         cb_ref, lg_ref, lb_ref, cache_ref,
         q_ref, k_ref, v_ref, vn_ref, bc_ref, nc_ref, buf_ref) = refs
    ti = pl.program_id(1)
    m = bb * tt

    x = x_ref[...].reshape(m, D_MODEL)
    h = _rms(x, gpre_ref[...]).astype(_BF16)

    def proj(lo, hi):
        return jnp.dot(h, win_ref[:, lo:hi], preferred_element_type=_F32)

    c, s1, s2 = c_ref[...], s1_ref[...], s2_ref[...]
    q = _rope_pairs(proj(0, A_WIDTH), c, s1, s2) * (A_HEAD_DIM ** -0.5)
    q_ref[...] = q.astype(_BF16).reshape(bb, tt, A_WIDTH)
    k = _rope_pairs(proj(A_WIDTH, 2 * A_WIDTH), c, s1, s2)
    k_ref[...] = k.reshape(bb, tt, A_WIDTH)
    v = proj(2 * A_WIDTH, 3 * A_WIDTH)
    v_ref[...] = v.reshape(bb, tt, A_WIDTH)
    if prompt:
        kb_ref[...] = k.astype(_BF16).reshape(bb, tt, A_WIDTH)
        vb_ref[...] = v.astype(_BF16).reshape(bb, tt, A_WIDTH)

    o = 3 * A_WIDTH
    u = _gelu_tanh(proj(o, o + B_WIDTH))
    vn = _rms(_gelu_tanh(proj(o + B_WIDTH, o + 2 * B_WIDTH)), gn_ref[...])
    if not prompt:
        vn_ref[...] = vn.reshape(bb, tt, B_WIDTH)
    vnb = vn.astype(_BF16)
    row = lax.broadcasted_iota(jnp.int32, (chunk, chunk), 0)
    col = lax.broadcasted_iota(jnp.int32, (chunk, chunk), 1)
    tri = row >= col
    ws = jnp.concatenate(
        [jnp.where(tri, ws_ref[hd], 0.0).astype(_BF16) for hd in range(B_HEADS)], axis=0)
    lane = lax.broadcasted_iota(jnp.int32, (chunk, B_WIDTH), 1)
    bias = bias_ref[...]
    mixed_parts = []
    for r0 in range(0, m, chunk):
        mm = jnp.dot(ws, vnb[r0:r0 + chunk], preferred_element_type=_F32)
        mixed = mm[0:chunk]
        for hd in range(1, B_HEADS):
            mixed = jnp.where(lane >= hd * B_HEAD_DIM, mm[hd * chunk:(hd + 1) * chunk], mixed)
        mixed_parts.append(mixed + bias)
    b_out = u * jnp.concatenate(mixed_parts, axis=0)

    o = 3 * A_WIDTH + 2 * B_WIDTH
    glu = proj(o, o + C_WIDTH) * _sigmoid(proj(o + C_WIDTH, o + 2 * C_WIDTH))
    glu = glu.reshape(bb, tt, C_WIDTH)
    lo = HIST_PAD - HIST
    if prompt:
        @pl.when(ti == 0)
        def _():
            buf_ref[:, 0:HIST_PAD, :] = jnp.zeros((bb, HIST_PAD, C_WIDTH), _F32)

        @pl.when(ti > 0)
        def _():
            buf_ref[:, 0:HIST_PAD, :] = buf_ref[:, tt:tt + HIST_PAD, :]
    else:
        buf_ref[:, 0:lo, :] = jnp.zeros((bb, lo, C_WIDTH), _F32)
        buf_ref[:, lo:HIST_PAD, :] = cache_ref[...]
    buf_ref[:, HIST_PAD:HIST_PAD + tt, :] = glu
    cw = cw_ref[...]
    y = jnp.zeros((bb, tt, C_WIDTH), _F32) + cb_ref[...]
    for j in range(CONV_WIDTH):
        y = y + cw[j:j + 1, :] * buf_ref[:, lo + j:lo + j + tt, :]
    mu = jnp.mean(y, axis=-1, keepdims=True)
    yc = y - mu
    var = jnp.mean(yc * yc, axis=-1, keepdims=True)
    yn = yc * lax.rsqrt(var + EPS) * lg_ref[...] + lb_ref[...]
    c_out = (yn * _sigmoid(yn)).reshape(m, C_WIDTH)

    bc_ref[...] = jnp.concatenate([b_out, c_out], axis=1).astype(_BF16).reshape(
        bb, tt, B_WIDTH + C_WIDTH)

    @pl.when(ti == pl.num_programs(1) - 1)
    def _():
        nc_ref[...] = buf_ref[:, tt + lo:tt + HIST_PAD, :]


def _front(x, tabs, lw, cache_conv, *, prompt):
    bsz, t, _ = x.shape
    if prompt:
        bb, tt = 1, FRONT_ROWS
    else:
        bb, tt = FRONT_ROWS // t, t
    chunk = min(t, GMLP_CHUNK)
    m = bb * tt
    nt = t // tt
    full = lambda shape: pl.BlockSpec(shape, lambda b, i: (0,) * len(shape))
    act = lambda w: pl.BlockSpec((bb, tt, w), lambda b, i: (b, i, 0))
    tab = pl.BlockSpec((m, PAIR), lambda b, i: (i, 0))
    hist = pl.BlockSpec((bb, HIST, C_WIDTH), lambda b, i: (b, 0, 0))
    in_specs = [act(D_MODEL), tab, tab, tab, full((1, D_MODEL)), full((D_MODEL, D_IN)),
                full((1, B_WIDTH)), full((B_HEADS, chunk, chunk)), full((chunk, B_WIDTH)),
                full((CONV_WIDTH, C_WIDTH)), full((1, C_WIDTH)), full((1, C_WIDTH)),
                full((1, C_WIDTH))]
    args = [x, *tabs, lw['g_pre'], lw['w_in'], lw['gmlp_norm'], lw['w_s'], lw['bias'],
            lw['conv_w'], lw['conv_b'], lw['ln_g'], lw['ln_b']]
    sds = jax.ShapeDtypeStruct
    if prompt:
        out_shape = [sds((bsz, t, A_WIDTH), _BF16), sds((bsz, t, A_WIDTH), _F32),
                     sds((bsz, t, A_WIDTH), _F32), sds((bsz, t, A_WIDTH), _BF16),
                     sds((bsz, t, A_WIDTH), _BF16), sds((bsz, t, A_WIDTH), _BF16),
                     sds((bsz, HIST, C_WIDTH), _F32)]
        out_specs = [act(A_WIDTH)] * 6 + [hist]
    else:
        in_specs.append(hist)
        args.append(cache_conv)
        out_shape = [sds((bsz, t, A_WIDTH), _BF16), sds((bsz, t, A_WIDTH), _F32),
                     sds((bsz, t, A_WIDTH), _F32), sds((bsz, t, B_WIDTH), _F32),
                     sds((bsz, t, A_WIDTH), _BF16), sds((bsz, HIST, C_WIDTH), _F32)]
        out_specs = [act(A_WIDTH)] * 3 + [act(B_WIDTH), act(A_WIDTH), hist]
    return pl.pallas_call(
        functools.partial(_front_kernel, bb=bb, tt=tt, chunk=chunk, prompt=prompt),
        grid=(bsz // bb, nt),
        in_specs=in_specs,
        out_specs=out_specs,
        out_shape=out_shape,
        scratch_shapes=[pltpu.VMEM((bb, HIST_PAD + tt, C_WIDTH), _F32)],
        compiler_params=pltpu.CompilerParams(
            dimension_semantics=("arbitrary", "arbitrary"), vmem_limit_bytes=VMEM_LIMIT_BYTES),
        name="front_prompt" if prompt else "front_sample",
    )(*args)


def _diff_lambda(lam_ref, lam_init):
    lq = lam_ref[...]
    a = jnp.sum(lq[0:1] * lq[1:2], axis=-1, keepdims=True)
    b = jnp.sum(lq[2:3] * lq[3:4], axis=-1, keepdims=True)
    return jnp.exp(a) - jnp.exp(b) + lam_init


def _stack_query(q, n):
    lane = lax.broadcasted_iota(jnp.int32, (n, PAIR), 1)
    zero = jnp.zeros_like(q)
    return jnp.concatenate([jnp.where(lane < A_HEAD_DIM, q, zero),
                            jnp.where(lane >= A_HEAD_DIM, q, zero)], axis=0)


def _softmax_step(s, vt, m_prev, l_prev, acc_prev):
    m_new = jnp.maximum(m_prev, jnp.max(s, axis=-1, keepdims=True))
    alpha = jnp.exp(m_prev - m_new)
    p = jnp.exp(s - m_new)
    l_new = alpha * l_prev + jnp.sum(p, axis=-1, keepdims=True)
    acc_new = alpha * acc_prev + jnp.dot(p.astype(_BF16), vt, preferred_element_type=_F32)
    return m_new, l_new, acc_new


def _finish_head(acc, l, n, lam, subln, lam_init):
    o = acc / l
    d = o[0:n] - lam * o[n:2 * n]
    return _rms(d, subln) * (1.0 - lam_init)


_NT = (((1,), (1,)), ((), ()))


def _attn_prompt_kernel(lam_ref, sub_ref, q_ref, k_ref, v_ref, o_ref, qm_ref, m_ref, l_ref,
                        acc_ref, *, tq, lam_init):
    qi = pl.program_id(2)
    ki = pl.program_id(3)

    @pl.when(ki == 0)
    def _():
        qm_ref[...] = _stack_query(q_ref[0], tq)
        m_ref[...] = jnp.full(m_ref.shape, NEG_INF, _F32)
        l_ref[...] = jnp.zeros(l_ref.shape, _F32)
        acc_ref[...] = jnp.zeros(acc_ref.shape, _F32)

    def step(masked):
        s = lax.dot_general(qm_ref[...], k_ref[0], _NT, preferred_element_type=_F32)
        if masked:
            row = lax.broadcasted_iota(jnp.int32, (2 * tq, tq), 0)
            col = lax.broadcasted_iota(jnp.int32, (2 * tq, tq), 1)
            s = jnp.where((col >> CHUNK_SHIFT) <= ((row & (tq - 1)) >> CHUNK_SHIFT), s, NEG_INF)
        m_new, l_new, acc_new = _softmax_step(s, v_ref[0], m_ref[...], l_ref[...], acc_ref[...])
        m_ref[...] = m_new
        l_ref[...] = l_new
        acc_ref[...] = acc_new

    @pl.when(ki < qi)
    def _():
        step(False)

    @pl.when(ki == qi)
    def _():
        step(True)
        lam = _diff_lambda(lam_ref, lam_init)
        o_ref[0] = _finish_head(acc_ref[...], l_ref[...], tq, lam, sub_ref[...],
                                lam_init).astype(_BF16)


def _attn_prompt(q, kb, vb, lam_qk, subln, lam_init):
    bsz, t, _ = q.shape
    tq = ATTN_TILE
    nq = t // tq
    qspec = pl.BlockSpec((1, tq, PAIR), lambda b, j, qi, ki: (b, qi, j))
    kspec = pl.BlockSpec((1, tq, PAIR), lambda b, j, qi, ki: (b, jnp.minimum(ki, qi), j))
    return pl.pallas_call(
        functools.partial(_attn_prompt_kernel, tq=tq, lam_init=lam_init),
        grid=(bsz, A_HEADS, nq, nq),
        in_specs=[pl.BlockSpec((4, A_HEAD_DIM), lambda b, j, qi, ki: (0, 0)),
                  pl.BlockSpec((1, PAIR), lambda b, j, qi, ki: (0, 0)),
                  qspec, kspec, kspec],
        out_specs=qspec,
        out_shape=jax.ShapeDtypeStruct((bsz, t, A_WIDTH), _BF16),
        scratch_shapes=[pltpu.VMEM((2 * tq, PAIR), _BF16), pltpu.VMEM((2 * tq, 1), _F32),
                        pltpu.VMEM((2 * tq, 1), _F32), pltpu.VMEM((2 * tq, PAIR), _F32)],
        compiler_params=pltpu.CompilerParams(
            dimension_semantics=("arbitrary",) * 4, vmem_limit_bytes=VMEM_LIMIT_BYTES),
        name="attn_prompt",
    )(lam_qk, subln, q, kb, vb)


def _attn_sample_kernel(lam_ref, sub_ref, q_ref, kn_ref, vn_ref, kc_ref, vc_ref, o_ref,
                        qm_ref, m_ref, l_ref, acc_ref, *, t, lam_init):
    ki = pl.program_id(1)

    def update(kt, vt, first):
        for j in range(A_HEADS):
            sl = slice(j * PAIR, (j + 1) * PAIR)
            s = lax.dot_general(qm_ref[j], kt[:, sl].astype(_BF16), _NT,
                                preferred_element_type=_F32)
            if first:
                m_prev = jnp.full((2 * t, 1), NEG_INF, _F32)
                l_prev = jnp.zeros((2 * t, 1), _F32)
                acc_prev = jnp.zeros((2 * t, PAIR), _F32)
            else:
                m_prev, l_prev, acc_prev = m_ref[j], l_ref[j], acc_ref[j]
            m_new, l_new, acc_new = _softmax_step(s, vt[:, sl].astype(_BF16), m_prev, l_prev,
                                                  acc_prev)
            m_ref[j] = m_new
            l_ref[j] = l_new
            acc_ref[j] = acc_new

    @pl.when(ki == 0)
    def _():
        q = q_ref[0]
        for j in range(A_HEADS):
            qm_ref[j] = _stack_query(q[:, j * PAIR:(j + 1) * PAIR], t)
        update(kn_ref[0], vn_ref[0], True)

    update(kc_ref[...], vc_ref[...], False)

    @pl.when(ki == pl.num_programs(1) - 1)
    def _():
        lam = _diff_lambda(lam_ref, lam_init)
        for j in range(A_HEADS):
            o_ref[0, :, j * PAIR:(j + 1) * PAIR] = _finish_head(
                acc_ref[j], l_ref[j], t, lam, sub_ref[...], lam_init).astype(_BF16)


def _attn_sample(q, k_new, v_new, cache_k, cache_v, layer, lam_qk, subln, lam_init):
    bsz, t, _ = q.shape
    past = cache_k.shape[2]
    tk = CACHE_TILE
    new = pl.BlockSpec((1, t, A_WIDTH), lambda b, ki: (b, 0, 0))
    cache = pl.BlockSpec((None, None, tk, A_WIDTH), lambda b, ki: (layer, b, ki, 0))
    return pl.pallas_call(
        functools.partial(_attn_sample_kernel, t=t, lam_init=lam_init),
        grid=(bsz, past // tk),
        in_specs=[pl.BlockSpec((4, A_HEAD_DIM), lambda b, ki: (0, 0)),
                  pl.BlockSpec((1, PAIR), lambda b, ki: (0, 0)),
                  new, new, new, cache, cache],
        out_specs=new,
        out_shape=jax.ShapeDtypeStruct((bsz, t, A_WIDTH), _BF16),
        scratch_shapes=[pltpu.VMEM((A_HEADS, 2 * t, PAIR), _BF16),
                        pltpu.VMEM((A_HEADS, 2 * t, 1), _F32),
                        pltpu.VMEM((A_HEADS, 2 * t, 1), _F32),
                        pltpu.VMEM((A_HEADS, 2 * t, PAIR), _F32)],
        compiler_params=pltpu.CompilerParams(
            dimension_semantics=("arbitrary", "arbitrary"), vmem_limit_bytes=VMEM_LIMIT_BYTES),
        name="attn_sample",
    )(lam_qk, subln, q, k_new, v_new, cache_k, cache_v)


def _back_kernel(x_ref, a_ref, bc_ref, wout_ref, gpost_ref, gpre_ref, wup_ref, wdown_ref,
                 gmlp_ref, o_ref):
    mix = jnp.dot(a_ref[...], wout_ref[0:A_WIDTH, :], preferred_element_type=_F32)
    mix = mix + jnp.dot(bc_ref[...], wout_ref[A_WIDTH:, :], preferred_element_type=_F32)
    x1 = x_ref[...] + _rms(mix, gpost_ref[...])
    h = _rms(x1, gpre_ref[...]).astype(_BF16)
    f = jnp.zeros(x1.shape, _F32)
    for c0 in range(0, D_FF, FF_CHUNK):
        up = jnp.dot(h, wup_ref[:, c0:c0 + FF_CHUNK], preferred_element_type=_F32)
        up = jnp.maximum(up, 0.0)
        f = f + jnp.dot((up * up).astype(_BF16), wdown_ref[c0:c0 + FF_CHUNK, :],
                        preferred_element_type=_F32)
    o_ref[...] = x1 + _rms(f, gmlp_ref[...])


def _back(x, a, bc, lw):
    m = x.shape[0]
    tm = BACK_ROWS
    full = lambda shape: pl.BlockSpec(shape, lambda i: (0, 0))
    rows = lambda w: pl.BlockSpec((tm, w), lambda i: (i, 0))
    return pl.pallas_call(
        _back_kernel,
        grid=(m // tm,),
        in_specs=[rows(D_MODEL), rows(A_WIDTH), rows(A_WIDTH), full((D_MODEL, D_MODEL)),
                  full((1, D_MODEL)), full((1, D_MODEL)), full((D_MODEL, D_FF)),
                  full((D_FF, D_MODEL)), full((1, D_MODEL))],
        out_specs=rows(D_MODEL),
        out_shape=jax.ShapeDtypeStruct((m, D_MODEL), _F32),
        compiler_params=pltpu.CompilerParams(
            dimension_semantics=("arbitrary",), vmem_limit_bytes=VMEM_LIMIT_BYTES),
        name="back",
    )(x, a, bc, lw['w_out'], lw['g_post'], lw['g_mlp_pre'], lw['w_up'], lw['w_down'],
      lw['g_mlp_post'])


def _rope_tables(pos0, t, reps):
    half = ROT_DIM // 2
    freqs = ROPE_THETA ** (-jnp.arange(0, ROT_DIM, 2, dtype=_F32) / ROT_DIM)
    pos = (pos0 + jnp.arange(t)).astype(_F32)
    ang = pos[:, None] * freqs[None, :]
    cos, sin = jnp.cos(ang), jnp.sin(ang)
    pad = A_HEAD_DIM - ROT_DIM
    one = jnp.ones((t, pad), _F32)
    zero = jnp.zeros((t, pad), _F32)
    zh = jnp.zeros((t, half), _F32)
    c = jnp.concatenate([cos, cos, one], axis=1)
    s1 = jnp.concatenate([zh, sin, zero], axis=1)
    s2 = jnp.concatenate([-sin, zh, zero], axis=1)
    return tuple(jnp.tile(jnp.tile(a, (1, 2)), (reps, 1)) for a in (c, s1, s2))


def _layer(x, layer, lw, tabs, cache_k, cache_v, cache_conv):
    bsz, t, _ = x.shape
    prompt = cache_k is None
    lam_init = 0.8 - 0.6 * math.exp(-0.3 * layer)
    if prompt:
        q, k, v, kb, vb, bc, new_conv = _front(x, tabs, lw, None, prompt=True)
        a = _attn_prompt(q, kb, vb, lw['lam_qk'], lw['subln'], lam_init)
        vn = None
    else:
        q, k, v, vn, bc, new_conv = _front(x, tabs, lw, cache_conv[layer], prompt=False)
        a = _attn_sample(q, k, v, cache_k, cache_v, layer, lw['lam_qk'], lw['subln'], lam_init)
    y = _back(x.reshape(bsz * t, D_MODEL), a.reshape(bsz * t, A_WIDTH),
              bc.reshape(bsz * t, A_WIDTH), lw)
    return y.reshape(bsz, t, D_MODEL), k, v, new_conv, vn


def kernel(x_prompt, x_sample, cache_k, cache_v, cache_conv, norm_mix_pre, norm_mix_post,
           norm_mlp_pre, norm_mlp_post, w_in, diff_lambda, diff_subln, gmlp_norm, gmlp_w_s,
           gmlp_bias, conv_w, conv_b, conv_ln_gain, conv_ln_bias, w_out, w_up, w_down):
    depth = w_in.shape[0]
    bp, tp, _ = x_prompt.shape
    bs, ts, _ = x_sample.shape
    past = cache_k.shape[2]
    cache_k = cache_k.reshape(depth, bs, past, A_WIDTH)
    cache_v = cache_v.reshape(depth, bs, past, A_WIDTH)
    tabs_p = _rope_tables(0, tp, 1)
    tabs_s = _rope_tables(past, ts, FRONT_ROWS // ts)
    w_in_b, w_out_b = w_in.astype(_BF16), w_out.astype(_BF16)
    w_up_b, w_down_b = w_up.astype(_BF16), w_down.astype(_BF16)

    def layer_weights(l, chunk):
        return dict(
            g_pre=norm_mix_pre[l][None], g_post=norm_mix_post[l][None],
            g_mlp_pre=norm_mlp_pre[l][None], g_mlp_post=norm_mlp_post[l][None],
            w_in=w_in_b[l], lam_qk=diff_lambda[l], subln=diff_subln[l][None],
            gmlp_norm=gmlp_norm[l][None], w_s=gmlp_w_s[l][:, :chunk, :chunk],
            bias=jnp.repeat(gmlp_bias[l][:, :chunk].T, B_HEAD_DIM, axis=1),
            conv_w=conv_w[l], conv_b=conv_b[l][None], ln_g=conv_ln_gain[l][None],
            ln_b=conv_ln_bias[l][None], w_out=w_out_b[l], w_up=w_up_b[l], w_down=w_down_b[l])

    xp, xs = x_prompt, x_sample
    outs = [[] for _ in range(7)]
    for l in range(depth):
        xp, k_p, v_p, c_p, _ = _layer(xp, l, layer_weights(l, min(tp, GMLP_CHUNK)), tabs_p,
                                      None, None, None)
        xs, k_s, v_s, c_s, g_s = _layer(xs, l, layer_weights(l, min(ts, GMLP_CHUNK)), tabs_s,
                                        cache_k, cache_v, cache_conv)
        for lst, val in zip(outs, (k_p, v_p, c_p, k_s, v_s, c_s, g_s)):
            lst.append(val)
    kp, vp, cp, ksl, vsl, csl, gsl = (jnp.stack(o) for o in outs)
    return (xp, xs,
            kp.reshape(depth, bp, tp, 2 * A_HEADS, A_HEAD_DIM),
            vp.reshape(depth, bp, tp, A_HEADS, 2 * A_HEAD_DIM),
            cp,
            ksl.reshape(depth, bs, ts, 2 * A_HEADS, A_HEAD_DIM),
            vsl.reshape(depth, bs, ts, A_HEADS, 2 * A_HEAD_DIM),
            csl,
            gsl.reshape(depth, bs, ts, B_HEADS, B_HEAD_DIM))
```

```python
import functools
import math

import jax
import jax.numpy as jnp
from jax import lax
from jax.experimental import pallas as pl
from jax.experimental.pallas import tpu as pltpu

D_MODEL = 1024
A_HEADS = 4
A_HEAD_DIM = 64
A_WIDTH = A_HEADS * 2 * A_HEAD_DIM
PAIR = 2 * A_HEAD_DIM
ROT_DIM = A_HEAD_DIM // 4
ROPE_THETA = 500000.0
B_HEADS = 4
B_HEAD_DIM = 64
B_WIDTH = B_HEADS * B_HEAD_DIM
GMLP_CHUNK = 128
C_WIDTH = 256
CONV_WIDTH = 31
HIST = CONV_WIDTH - 1
SUBLANES = 8
HIST_PAD = 32
D_IN = 3 * A_WIDTH + 2 * B_WIDTH + 2 * C_WIDTH
D_FF = 4 * D_MODEL
EPS = 1e-6
NEG_INF = -1e30
LOG2_E = math.log2(math.e)
CHUNK = 64
CHUNK_SHIFT = CHUNK.bit_length() - 1

VMEM_LIMIT_BYTES = 56 * 1024 * 1024
ATTN_TILE = 512
FRONT_ROWS = ATTN_TILE
ATTN_ROWS = 256
CACHE_TILE = 1024
BACK_ROWS = 512
FF_CHUNK = 1024

_BF16 = jnp.bfloat16
_F32 = jnp.float32
_NT = (((1,), (1,)), ((), ()))


def _rms(x, g):
    return x * lax.rsqrt(jnp.mean(x * x, axis=-1, keepdims=True) + EPS) * g


def _gelu_tanh(x):
    c = math.sqrt(2.0 / math.pi)
    return 0.5 * x * (1.0 + jnp.tanh(c * (x + 0.044715 * (x * x * x))))


def _sigmoid(x):
    return 1.0 / (1.0 + jnp.exp(-x))


def _rope_pairs(z, c, s1, s2, axis):
    outs = []
    for j in range(A_WIDTH // PAIR):
        zj = z[:, j * PAIR:(j + 1) * PAIR] if axis == 1 else z[j * PAIR:(j + 1) * PAIR, :]
        up = pltpu.roll(zj, ROT_DIM // 2, axis=axis)
        dn = pltpu.roll(zj, PAIR - ROT_DIM // 2, axis=axis)
        outs.append(zj * c + up * s1 + dn * s2)
    return jnp.concatenate(outs, axis=axis)


_FRONT_COMMON = ('x', 'c', 's1', 's2', 'gpre', 'win', 'gn', 'ws', 'bias', 'cw', 'cb', 'lg', 'lb')


def _front_kernel(*refs, bb, tt, chunk, prompt, n_alias):
    if prompt:
        names = _FRONT_COMMON + ('ct', 's1t', 's2t', 'wkt') + ('alias',) * n_alias + (
            'q', 'kt', 'v4', 'kbt', 'vb', 'bc', 'nc', 'buf', 'xs')
    else:
        names = _FRONT_COMMON + ('cache', 'q', 'k', 'v', 'vn', 'bc', 'nc', 'buf', 'xs')
    r = dict(zip(names, refs))
    ti = pl.program_id(1)
    m = bb * tt

    x = r['x'][...].reshape(m, D_MODEL)
    h = _rms(x, r['gpre'][...]).astype(_BF16)
    win_ref = r['win']

    def proj(lo, hi):
        return jnp.dot(h, win_ref[:, lo:hi], preferred_element_type=_F32)

    c, s1, s2 = r['c'][...], r['s1'][...], r['s2'][...]
    q = _rope_pairs(proj(0, A_WIDTH), c, s1, s2, 1) * (A_HEAD_DIM ** -0.5 * LOG2_E)
    r['q'][...] = q.astype(_BF16).reshape(bb, tt, A_WIDTH)
    v = proj(2 * A_WIDTH, 3 * A_WIDTH)
    if prompt:
        kt = lax.dot_general(r['wkt'][...], h, _NT, preferred_element_type=_F32)
        kt = _rope_pairs(kt, r['ct'][...], r['s1t'][...], r['s2t'][...], 0)
        r['kt'][0] = kt
        r['kbt'][0, 0] = kt.astype(_BF16)
        for j in range(A_HEADS):
            r['v4'][0, pl.ds(j, tt, stride=A_HEADS), :] = v[:, j * PAIR:(j + 1) * PAIR]
        r['vb'][...] = v.astype(_BF16).reshape(bb, tt, A_WIDTH)
    else:
        k = _rope_pairs(proj(A_WIDTH, 2 * A_WIDTH), c, s1, s2, 1)
        r['k'][...] = k.reshape(bb, tt, A_WIDTH)
        r['v'][...] = v.reshape(bb, tt, A_WIDTH)

    o = 3 * A_WIDTH
    u = _gelu_tanh(proj(o, o + B_WIDTH))
    vn = _rms(_gelu_tanh(proj(o + B_WIDTH, o + 2 * B_WIDTH)), r['gn'][...])
    if not prompt:
        r['vn'][...] = vn.reshape(bb, tt, B_WIDTH)
    vnb = vn.astype(_BF16)
    row = lax.broadcasted_iota(jnp.int32, (chunk, chunk), 0)
    col = lax.broadcasted_iota(jnp.int32, (chunk, chunk), 1)
    tri = row >= col
    ws = jnp.concatenate(
        [jnp.where(tri, r['ws'][hd], 0.0).astype(_BF16) for hd in range(B_HEADS)], axis=0)
    lane = lax.broadcasted_iota(jnp.int32, (chunk, B_WIDTH), 1)
    bias = r['bias'][...]
    mixed_parts = []
    for r0 in range(0, m, chunk):
        mm = jnp.dot(ws, vnb[r0:r0 + chunk], preferred_element_type=_F32)
        mixed = mm[0:chunk]
        for hd in range(1, B_HEADS):
            mixed = jnp.where(lane >= hd * B_HEAD_DIM, mm[hd * chunk:(hd + 1) * chunk], mixed)
        mixed_parts.append(mixed + bias)
    b_out = u * jnp.concatenate(mixed_parts, axis=0)

    o = 3 * A_WIDTH + 2 * B_WIDTH
    glu = proj(o, o + C_WIDTH) * _sigmoid(proj(o + C_WIDTH, o + 2 * C_WIDTH))
    glu = glu.reshape(bb, tt, C_WIDTH)
    buf_ref, xs_ref = r['buf'], r['xs']
    lo = HIST_PAD - HIST
    if prompt:
        @pl.when(ti == 0)
        def _():
            buf_ref[:, 0:HIST_PAD, :] = jnp.zeros((bb, HIST_PAD, C_WIDTH), _F32)

        @pl.when(ti > 0)
        def _():
            buf_ref[:, 0:HIST_PAD, :] = buf_ref[:, tt:tt + HIST_PAD, :]
    else:
        buf_ref[:, 0:lo, :] = jnp.zeros((bb, lo, C_WIDTH), _F32)
        buf_ref[:, lo:HIST_PAD, :] = r['cache'][...]
    buf_ref[:, HIST_PAD:HIST_PAD + tt, :] = glu
    for p in range(SUBLANES):
        n = tt + SUBLANES * ((CONV_WIDTH - 1 - p) // SUBLANES)
        xs_ref[p, :, 0:n, :] = buf_ref[:, lo + p:lo + p + n, :]
    cw = r['cw'][...]
    y = jnp.zeros((bb, tt, C_WIDTH), _F32) + r['cb'][...]
    for j in range(CONV_WIDTH):
        a, p = divmod(j, SUBLANES)
        y = y + cw[j:j + 1, :] * xs_ref[p, :, SUBLANES * a:SUBLANES * a + tt, :]
    mu = jnp.mean(y, axis=-1, keepdims=True)
    yc = y - mu
    var = jnp.mean(yc * yc, axis=-1, keepdims=True)
    yn = yc * lax.rsqrt(var + EPS) * r['lg'][...] + r['lb'][...]
    c_out = (yn * _sigmoid(yn)).reshape(m, C_WIDTH)

    r['bc'][...] = jnp.concatenate([b_out, c_out], axis=1).astype(_BF16).reshape(
        bb, tt, B_WIDTH + C_WIDTH)

    @pl.when(ti == pl.num_programs(1) - 1)
    def _():
        r['nc'][...] = buf_ref[:, tt + lo:tt + HIST_PAD, :]


def _front(x, tabs, lw, *, layer=0, depth=1, tabs_t=None, kv_all=None, cache_conv=None):
    bsz, t, _ = x.shape
    prompt = tabs_t is not None
    if prompt:
        bb, tt = 1, FRONT_ROWS
    else:
        bb, tt = FRONT_ROWS // t, t
    chunk = min(t, GMLP_CHUNK)
    m = bb * tt
    nt = t // tt
    full = lambda shape: pl.BlockSpec(shape, lambda b, i: (0,) * len(shape))
    act = lambda w: pl.BlockSpec((bb, tt, w), lambda b, i: (b, i, 0))
    tab = pl.BlockSpec((m, PAIR), lambda b, i: (i, 0))
    hist = pl.BlockSpec((bb, HIST, C_WIDTH), lambda b, i: (b, 0, 0))
    in_specs = [act(D_MODEL), tab, tab, tab, full((1, D_MODEL)), full((D_MODEL, D_IN)),
                full((1, B_WIDTH)), full((B_HEADS, chunk, chunk)), full((chunk, B_WIDTH)),
                full((CONV_WIDTH, C_WIDTH)), full((1, C_WIDTH)), full((1, C_WIDTH)),
                full((1, C_WIDTH))]
    args = [x, *tabs, lw['g_pre'], lw['w_in'], lw['gmlp_norm'], lw['w_s'], lw['bias'],
            lw['conv_w'], lw['conv_b'], lw['ln_g'], lw['ln_b']]
    sds = jax.ShapeDtypeStruct
    aliases = {}
    if prompt:
        tab_t = pl.BlockSpec((PAIR, tt), lambda b, i: (0, i))
        in_specs += [tab_t, tab_t, tab_t, full((A_WIDTH, D_MODEL))]
        args += [*tabs_t, lw['w_k_t']]
        if kv_all is not None:
            aliases = {len(args): 1, len(args) + 1: 2}
            in_specs += [pl.BlockSpec(memory_space=pl.ANY)] * 2
            args += list(kv_all)
        out_shape = [sds((bsz, t, A_WIDTH), _BF16),
                     sds((depth, bsz, A_WIDTH, t), _F32),
                     sds((depth, bsz, t * A_HEADS, PAIR), _F32),
                     sds((bsz, nt, A_WIDTH, tt), _BF16),
                     sds((bsz, t, A_WIDTH), _BF16), sds((bsz, t, A_WIDTH), _BF16),
                     sds((bsz, HIST, C_WIDTH), _F32)]
        out_specs = [act(A_WIDTH),
                     pl.BlockSpec((None, 1, A_WIDTH, tt), lambda b, i: (layer, b, 0, i)),
                     pl.BlockSpec((None, 1, tt * A_HEADS, PAIR), lambda b, i: (layer, b, i, 0)),
                     pl.BlockSpec((1, 1, A_WIDTH, tt), lambda b, i: (b, i, 0, 0)),
                     act(A_WIDTH), act(A_WIDTH), hist]
    else:
        in_specs.append(hist)
        args.append(cache_conv)
        out_shape = [sds((bsz, t, A_WIDTH), _BF16), sds((bsz, t, A_WIDTH), _F32),
                     sds((bsz, t, A_WIDTH), _F32), sds((bsz, t, B_WIDTH), _F32),
                     sds((bsz, t, A_WIDTH), _BF16), sds((bsz, HIST, C_WIDTH), _F32)]
        out_specs = [act(A_WIDTH)] * 3 + [act(B_WIDTH), act(A_WIDTH), hist]
    return pl.pallas_call(
        functools.partial(_front_kernel, bb=bb, tt=tt, chunk=chunk, prompt=prompt,
                          n_alias=len(aliases)),
        grid=(bsz // bb, nt),
        in_specs=in_specs,
        out_specs=out_specs,
        out_shape=out_shape,
        input_output_aliases=aliases,
        scratch_shapes=[pltpu.VMEM((bb, HIST_PAD + tt, C_WIDTH), _F32),
                        pltpu.VMEM((SUBLANES, bb, tt + HIST_PAD - SUBLANES, C_WIDTH), _F32)],
        compiler_params=pltpu.CompilerParams(
            dimension_semantics=("arbitrary", "arbitrary"), vmem_limit_bytes=VMEM_LIMIT_BYTES),
        name="front_prompt" if prompt else "front_sample",
    )(*args)


def _diff_lambda(lam_ref, lam_init):
    lq = lam_ref[...]
    a = jnp.sum(lq[0:1] * lq[1:2], axis=-1, keepdims=True)
    b = jnp.sum(lq[2:3] * lq[3:4], axis=-1, keepdims=True)
    return jnp.exp(a) - jnp.exp(b) + lam_init


def _stack_query(q, n):
    lane = lax.broadcasted_iota(jnp.int32, (n, PAIR), 1)
    zero = jnp.zeros_like(q)
    return jnp.concatenate([jnp.where(lane < A_HEAD_DIM, q, zero),
                            jnp.where(lane >= A_HEAD_DIM, q, zero)], axis=0)


def _finish_head(acc, l, n, lam, subln, lam_init):
    o = acc / l
    d = o[0:n] - lam * o[n:2 * n]
    return _rms(d, subln) * (1.0 - lam_init)


def _online_rows(blocks, vt, m_prev, l_prev, acc_prev):
    m_cur = jnp.max(functools.reduce(jnp.maximum, blocks), axis=-1, keepdims=True)
    if m_prev is None:
        m_new = jnp.broadcast_to(m_cur, blocks[0].shape)
    else:
        m_new = jnp.maximum(m_prev, m_cur)
        alpha = jnp.exp2(m_prev - m_new)
    ps = [jnp.exp2(blk - m_new) for blk in blocks]
    l_new = functools.reduce(jnp.add, ps)
    p = jnp.concatenate([x.astype(_BF16) for x in ps], axis=1)
    acc_new = jnp.dot(p, vt, preferred_element_type=_F32)
    if m_prev is not None:
        l_new = alpha * l_prev + l_new
        acc_new = alpha * acc_prev + acc_new
    return m_new, l_new, acc_new


def _attn_prompt_kernel(lam_ref, sub_ref, q_ref, k_ref, v_ref, o_ref, qm_ref, m_ref, l_ref,
                        acc_ref, *, tq, rows, lam_init):
    qi = pl.program_id(2)
    qm_ref[...] = _stack_query(q_ref[0], tq)

    def tile(ki, diagonal):
        kt = k_ref[0, ki]
        vt = v_ref[0, pl.ds(pl.multiple_of(ki * tq, tq), tq), :]
        chunks = range(0, 2 * tq, rows)
        nkeys = [(r & (tq - 1)) + rows if diagonal else tq for r in chunks]
        scores = [jnp.dot(qm_ref[r:r + rows], kt[:, 0:n], preferred_element_type=_F32)
                  for r, n in zip(chunks, nkeys)]
        for r, n, s in zip(chunks, nkeys, scores):
            sl = slice(r, r + rows)
            q0 = r & (tq - 1)
            blocks = []
            for c in range(0, n, PAIR):
                blk = s[:, c:c + PAIR]
                if diagonal and c + PAIR > q0:
                    row = lax.broadcasted_iota(jnp.int32, (rows, PAIR), 0) + q0
                    col = lax.broadcasted_iota(jnp.int32, (rows, PAIR), 1) + c
                    blk = jnp.where((col >> CHUNK_SHIFT) <= (row >> CHUNK_SHIFT), blk, NEG_INF)
                blocks.append(blk)
            prev = (None, None, None) if diagonal else (m_ref[sl], l_ref[sl], acc_ref[sl])
            m_ref[sl], l_ref[sl], acc_ref[sl] = _online_rows(blocks, vt[0:n], *prev)

    def body(ki, carry):
        tile(ki, False)
        return carry

    tile(qi, True)
    lax.fori_loop(0, qi, body, 0)
    lam = _diff_lambda(lam_ref, lam_init)
    l = jnp.sum(l_ref[...], axis=-1, keepdims=True)
    o_ref[0] = _finish_head(acc_ref[...], l, tq, lam, sub_ref[...], lam_init).astype(_BF16)


def _attn_prompt(q, kbt, vb, lam_qk, subln, lam_init):
    bsz, t, _ = q.shape
    tq = ATTN_TILE
    nt = t // tq
    qspec = pl.BlockSpec((1, tq, PAIR), lambda b, j, qi: (b, qi, j))
    kspec = pl.BlockSpec((1, nt, PAIR, tq), lambda b, j, qi: (b, 0, j, 0))
    vspec = pl.BlockSpec((1, t, PAIR), lambda b, j, qi: (b, 0, j))
    stat = pltpu.VMEM((2 * tq, PAIR), _F32)
    return pl.pallas_call(
        functools.partial(_attn_prompt_kernel, tq=tq, rows=ATTN_ROWS, lam_init=lam_init),
        grid=(bsz, A_HEADS, nt),
        in_specs=[pl.BlockSpec((4, A_HEAD_DIM), lambda b, j, qi: (0, 0)),
                  pl.BlockSpec((1, PAIR), lambda b, j, qi: (0, 0)),
                  qspec, kspec, vspec],
        out_specs=qspec,
        out_shape=jax.ShapeDtypeStruct((bsz, t, A_WIDTH), _BF16),
        scratch_shapes=[pltpu.VMEM((2 * tq, PAIR), _BF16), stat, stat, stat],
        compiler_params=pltpu.CompilerParams(
            dimension_semantics=("arbitrary",) * 3, vmem_limit_bytes=VMEM_LIMIT_BYTES),
        name="attn_prompt",
    )(lam_qk, subln, q, kbt, vb)


def _attn_sample_kernel(lam_ref, sub_ref, q_ref, kn_ref, vn_ref, kc_ref, vc_ref, o_ref,
                        qm_ref, m_ref, l_ref, acc_ref, *, t, lam_init):
    ki = pl.program_id(1)
    tk = kc_ref.shape[2]

    @pl.when(ki == 0)
    def _():
        q = q_ref[0]
        pad = jnp.zeros((PAIR - t, PAIR), _BF16)
        col = lax.broadcasted_iota(jnp.int32, (2 * t, PAIR), 1)
        for j in range(A_HEADS):
            sl = slice(j * PAIR, (j + 1) * PAIR)
            qm_ref[j] = _stack_query(q[:, sl], t)
            kj = jnp.concatenate([kn_ref[0, :, sl].astype(_BF16), pad], axis=0)
            vj = jnp.concatenate([vn_ref[0, :, sl].astype(_BF16), pad], axis=0)
            s = lax.dot_general(qm_ref[j], kj, _NT, preferred_element_type=_F32)
            s = jnp.where(col < t, s, NEG_INF)
            m_ref[j], l_ref[j], acc_ref[j] = _online_rows([s], vj, None, None, None)

    for j in range(A_HEADS):
        kj = kc_ref[2 * j:2 * j + 2].reshape(PAIR, tk).astype(_BF16)
        vj = vc_ref[pl.ds(j, tk, stride=A_HEADS), :].astype(_BF16)
        s = jnp.dot(qm_ref[j], kj, preferred_element_type=_F32)
        blocks = [s[:, c:c + PAIR] for c in range(0, tk, PAIR)]
        m_ref[j], l_ref[j], acc_ref[j] = _online_rows(blocks, vj, m_ref[j], l_ref[j], acc_ref[j])

    @pl.when(ki == pl.num_programs(1) - 1)
    def _():
        lam = _diff_lambda(lam_ref, lam_init)
        for j in range(A_HEADS):
            l = jnp.sum(l_ref[j], axis=-1, keepdims=True)
            o_ref[0, :, j * PAIR:(j + 1) * PAIR] = _finish_head(
                acc_ref[j], l, t, lam, sub_ref[...], lam_init).astype(_BF16)


def _attn_sample(q, k_new, v_new, cache_kt, cache_v4, layer, lam_qk, subln, lam_init):
    bsz, t, _ = q.shape
    past = cache_kt.shape[-1]
    tk = CACHE_TILE
    new = pl.BlockSpec((1, t, A_WIDTH), lambda b, ki: (b, 0, 0))
    kspec = pl.BlockSpec((None, None, 2 * A_HEADS, A_HEAD_DIM, tk),
                         lambda b, ki: (layer, b, 0, 0, ki))
    vspec = pl.BlockSpec((None, None, tk * A_HEADS, PAIR), lambda b, ki: (layer, b, ki, 0))
    stat = pltpu.VMEM((A_HEADS, 2 * t, PAIR), _F32)
    return pl.pallas_call(
        functools.partial(_attn_sample_kernel, t=t, lam_init=lam_init),
        grid=(bsz, past // tk),
        in_specs=[pl.BlockSpec((4, A_HEAD_DIM), lambda b, ki: (0, 0)),
                  pl.BlockSpec((1, PAIR), lambda b, ki: (0, 0)),
                  new, new, new, kspec, vspec],
        out_specs=new,
        out_shape=jax.ShapeDtypeStruct((bsz, t, A_WIDTH), _BF16),
        scratch_shapes=[pltpu.VMEM((A_HEADS, 2 * t, PAIR), _BF16), stat, stat, stat],
        compiler_params=pltpu.CompilerParams(
            dimension_semantics=("arbitrary", "arbitrary"), vmem_limit_bytes=VMEM_LIMIT_BYTES),
        name="attn_sample",
    )(lam_qk, subln, q, k_new, v_new, cache_kt, cache_v4)


def _back_kernel(x_ref, a_ref, bc_ref, wout_ref, gpost_ref, gpre_ref, wup_ref, wdown_ref,
                 gmlp_ref, o_ref):
    mix = jnp.dot(a_ref[...], wout_ref[0:A_WIDTH, :], preferred_element_type=_F32)
    mix = mix + jnp.dot(bc_ref[...], wout_ref[A_WIDTH:, :], preferred_element_type=_F32)
    x1 = x_ref[...] + _rms(mix, gpost_ref[...])
    h = _rms(x1, gpre_ref[...]).astype(_BF16)
    f = jnp.zeros(x1.shape, _F32)
    for c0 in range(0, D_FF, FF_CHUNK):
        up = jnp.dot(h, wup_ref[:, c0:c0 + FF_CHUNK], preferred_element_type=_F32)
        up = jnp.maximum(up, 0.0)
        f = f + jnp.dot((up * up).astype(_BF16), wdown_ref[c0:c0 + FF_CHUNK, :],
                        preferred_element_type=_F32)
    o_ref[...] = x1 + _rms(f, gmlp_ref[...])


def _back(x, a, bc, lw):
    m = x.shape[0]
    tm = BACK_ROWS
    full = lambda shape: pl.BlockSpec(shape, lambda i: (0, 0))
    rows = lambda w: pl.BlockSpec((tm, w), lambda i: (i, 0))
    return pl.pallas_call(
        _back_kernel,
        grid=(m // tm,),
        in_specs=[rows(D_MODEL), rows(A_WIDTH), rows(A_WIDTH), full((D_MODEL, D_MODEL)),
                  full((1, D_MODEL)), full((1, D_MODEL)), full((D_MODEL, D_FF)),
                  full((D_FF, D_MODEL)), full((1, D_MODEL))],
        out_specs=rows(D_MODEL),
        out_shape=jax.ShapeDtypeStruct((m, D_MODEL), _F32),
        compiler_params=pltpu.CompilerParams(
            dimension_semantics=("arbitrary",), vmem_limit_bytes=VMEM_LIMIT_BYTES),
        name="back",
    )(x, a, bc, lw['w_out'], lw['g_post'], lw['g_mlp_pre'], lw['w_up'], lw['w_down'],
      lw['g_mlp_post'])


def _rope_tables(pos0, t, reps):
    half = ROT_DIM // 2
    freqs = ROPE_THETA ** (-jnp.arange(0, ROT_DIM, 2, dtype=_F32) / ROT_DIM)
    pos = (pos0 + jnp.arange(t)).astype(_F32)
    ang = pos[:, None] * freqs[None, :]
    cos, sin = jnp.cos(ang), jnp.sin(ang)
    pad = A_HEAD_DIM - ROT_DIM
    one = jnp.ones((t, pad), _F32)
    zero = jnp.zeros((t, pad), _F32)
    zh = jnp.zeros((t, half), _F32)
    c = jnp.concatenate([cos, cos, one], axis=1)
    s1 = jnp.concatenate([zh, sin, zero], axis=1)
    s2 = jnp.concatenate([-sin, zh, zero], axis=1)
    return tuple(jnp.tile(jnp.tile(a, (1, 2)), (reps, 1)) for a in (c, s1, s2))


def kernel(x_prompt, x_sample, cache_k, cache_v, cache_conv, norm_mix_pre, norm_mix_post,
           norm_mlp_pre, norm_mlp_post, w_in, diff_lambda, diff_subln, gmlp_norm, gmlp_w_s,
           gmlp_bias, conv_w, conv_b, conv_ln_gain, conv_ln_bias, w_out, w_up, w_down):
    depth = w_in.shape[0]
    bp, tp, _ = x_prompt.shape
    bs, ts, _ = x_sample.shape
    past = cache_k.shape[2]
    cache_kt = jnp.transpose(cache_k, (0, 1, 3, 4, 2))
    cache_v4 = cache_v.reshape(depth, bs, past * A_HEADS, PAIR)
    tabs_p = _rope_tables(0, tp, 1)
    tabs_pt = tuple(a.T for a in tabs_p)
    tabs_s = _rope_tables(past, ts, FRONT_ROWS // ts)
    w_in_b, w_out_b = w_in.astype(_BF16), w_out.astype(_BF16)
    w_up_b, w_down_b = w_up.astype(_BF16), w_down.astype(_BF16)

    def layer_weights(l, chunk):
        return dict(
            g_pre=norm_mix_pre[l][None], g_post=norm_mix_post[l][None],
            g_mlp_pre=norm_mlp_pre[l][None], g_mlp_post=norm_mlp_post[l][None],
            w_in=w_in_b[l], w_k_t=w_in_b[l][:, A_WIDTH:2 * A_WIDTH].T,
            lam_qk=diff_lambda[l], subln=diff_subln[l][None],
            gmlp_norm=gmlp_norm[l][None], w_s=gmlp_w_s[l][:, :chunk, :chunk],
            bias=jnp.repeat(gmlp_bias[l][:, :chunk].T, B_HEAD_DIM, axis=1),
            conv_w=conv_w[l], conv_b=conv_b[l][None], ln_g=conv_ln_gain[l][None],
            ln_b=conv_ln_bias[l][None], w_out=w_out_b[l], w_up=w_up_b[l], w_down=w_down_b[l])

    xp, xs = x_prompt, x_sample
    kv_all = None
    outs = [[] for _ in range(5)]
    for l in range(depth):
        lam_init = 0.8 - 0.6 * math.exp(-0.3 * l)

        lw = layer_weights(l, min(tp, GMLP_CHUNK))
        q, kt_all, v4_all, kbt, vb, bc, c_p = _front(xp, tabs_p, lw, layer=l, depth=depth,
                                                   tabs_t=tabs_pt, kv_all=kv_all)
        kv_all = (kt_all, v4_all)
        a = _attn_prompt(q, kbt, vb, lw['lam_qk'], lw['subln'], lam_init)
        xp = _back(xp.reshape(bp * tp, D_MODEL), a.reshape(bp * tp, A_WIDTH),
                   bc.reshape(bp * tp, A_WIDTH), lw).reshape(bp, tp, D_MODEL)

        lw = layer_weights(l, min(ts, GMLP_CHUNK))
        q, k_s, v_s, g_s, bc, c_s = _front(xs, tabs_s, lw, cache_conv=cache_conv[l])
        a = _attn_sample(q, k_s, v_s, cache_kt, cache_v4, l, lw['lam_qk'], lw['subln'], lam_init)
        xs = _back(xs.reshape(bs * ts, D_MODEL), a.reshape(bs * ts, A_WIDTH),
                   bc.reshape(bs * ts, A_WIDTH), lw).reshape(bs, ts, D_MODEL)

        for lst, val in zip(outs, (c_p, k_s, v_s, c_s, g_s)):
            lst.append(val)
    cp, ksl, vsl, csl, gsl = (jnp.stack(o) for o in outs)
    kt_all, v4_all = kv_all
    kp = jnp.transpose(kt_all.reshape(depth, bp, 2 * A_HEADS, A_HEAD_DIM, tp), (0, 1, 4, 2, 3))
    return (xp, xs, kp,
            v4_all.reshape(depth, bp, tp, A_HEADS, 2 * A_HEAD_DIM),
            cp,
            ksl.reshape(depth, bs, ts, 2 * A_HEADS, A_HEAD_DIM),
            vsl.reshape(depth, bs, ts, A_HEADS, 2 * A_HEAD_DIM),
            csl,
            gsl.reshape(depth, bs, ts, B_HEADS, B_HEAD_DIM))
```

```python
import functools
import math

import jax
import jax.numpy as jnp
from jax import lax
from jax.experimental import pallas as pl
from jax.experimental.pallas import tpu as pltpu

D_MODEL = 1024
A_HEADS = 4
A_HEAD_DIM = 64
A_WIDTH = A_HEADS * 2 * A_HEAD_DIM
PAIR = 2 * A_HEAD_DIM
ROT_DIM = A_HEAD_DIM // 4
ROPE_THETA = 500000.0
B_HEADS = 4
B_HEAD_DIM = 64
B_WIDTH = B_HEADS * B_HEAD_DIM
GMLP_CHUNK = 128
C_WIDTH = 256
CONV_WIDTH = 31
HIST = CONV_WIDTH - 1
SUBLANES = 8
HIST_PAD = 32
D_IN = 3 * A_WIDTH + 2 * B_WIDTH + 2 * C_WIDTH
D_FF = 4 * D_MODEL
EPS = 1e-6
NEG_INF = -1e30
LOG2_E = math.log2(math.e)
CHUNK = 64
CHUNK_SHIFT = CHUNK.bit_length() - 1

VMEM_LIMIT_BYTES = 56 * 1024 * 1024
ATTN_TILE = 512
FRONT_ROWS = ATTN_TILE
ATTN_ROWS = 128
CACHE_TILE = 1024
BACK_ROWS = 512
FF_CHUNK = 1024

_BF16 = jnp.bfloat16
_F32 = jnp.float32
_NT = (((1,), (1,)), ((), ()))


def _rms(x, g):
    return x * lax.rsqrt(jnp.mean(x * x, axis=-1, keepdims=True) + EPS) * g


def _gelu_tanh(x):
    c = math.sqrt(2.0 / math.pi)
    return 0.5 * x * (1.0 + jnp.tanh(c * (x + 0.044715 * (x * x * x))))


def _sigmoid(x):
    return 1.0 / (1.0 + jnp.exp(-x))


def _rope_pairs(z, c, s1, s2, axis):
    outs = []
    for j in range(A_WIDTH // PAIR):
        zj = z[:, j * PAIR:(j + 1) * PAIR] if axis == 1 else z[j * PAIR:(j + 1) * PAIR, :]
        up = pltpu.roll(zj, ROT_DIM // 2, axis=axis)
        dn = pltpu.roll(zj, PAIR - ROT_DIM // 2, axis=axis)
        outs.append(zj * c + up * s1 + dn * s2)
    return jnp.concatenate(outs, axis=axis)


_FRONT_COMMON = ('x', 'c', 's1', 's2', 'gpre', 'win', 'gn', 'ws', 'bias', 'cw', 'cb', 'lg', 'lb')


def _front_kernel(*refs, bb, tt, chunk, prompt, n_alias):
    if prompt:
        names = _FRONT_COMMON + ('ct', 's1t', 's2t', 'wkt') + ('alias',) * n_alias + (
            'q', 'kt', 'v4', 'kbt', 'vb', 'bc', 'nc', 'buf', 'xs')
    else:
        names = _FRONT_COMMON + ('cache', 'q', 'k', 'v', 'vn', 'bc', 'nc', 'buf', 'xs')
    r = dict(zip(names, refs))
    ti = pl.program_id(1)
    m = bb * tt

    x = r['x'][...].reshape(m, D_MODEL)
    h = _rms(x, r['gpre'][...]).astype(_BF16)
    win_ref = r['win']

    def proj(lo, hi):
        return jnp.dot(h, win_ref[:, lo:hi], preferred_element_type=_F32)

    o = 3 * A_WIDTH + 2 * B_WIDTH
    glu = proj(o, o + C_WIDTH) * _sigmoid(proj(o + C_WIDTH, o + 2 * C_WIDTH))
    glu = glu.reshape(bb, tt, C_WIDTH)
    buf_ref, xs_ref = r['buf'], r['xs']
    lo = HIST_PAD - HIST
    if prompt:
        @pl.when(ti == 0)
        def _():
            buf_ref[:, 0:HIST_PAD, :] = jnp.zeros((bb, HIST_PAD, C_WIDTH), _F32)

        @pl.when(ti > 0)
        def _():
            buf_ref[:, 0:HIST_PAD, :] = buf_ref[:, tt:tt + HIST_PAD, :]
    else:
        buf_ref[:, 0:lo, :] = jnp.zeros((bb, lo, C_WIDTH), _F32)
        buf_ref[:, lo:HIST_PAD, :] = r['cache'][...]
    buf_ref[:, HIST_PAD:HIST_PAD + tt, :] = glu
    for p in range(SUBLANES):
        n = tt + SUBLANES * ((CONV_WIDTH - 1 - p) // SUBLANES)
        xs_ref[p, :, 0:n, :] = buf_ref[:, lo + p:lo + p + n, :]
    cw = r['cw'][...]
    y = jnp.zeros((bb, tt, C_WIDTH), _F32) + r['cb'][...]
    for j in range(CONV_WIDTH):
        a, p = divmod(j, SUBLANES)
        y = y + cw[j:j + 1, :] * xs_ref[p, :, SUBLANES * a:SUBLANES * a + tt, :]
    mu = jnp.mean(y, axis=-1, keepdims=True)
    yc = y - mu
    var = jnp.mean(yc * yc, axis=-1, keepdims=True)
    yn = yc * lax.rsqrt(var + EPS) * r['lg'][...] + r['lb'][...]
    c_out = (yn * _sigmoid(yn)).reshape(m, C_WIDTH)

    c, s1, s2 = r['c'][...], r['s1'][...], r['s2'][...]
    q = _rope_pairs(proj(0, A_WIDTH), c, s1, s2, 1) * (A_HEAD_DIM ** -0.5 * LOG2_E)
    r['q'][...] = q.astype(_BF16).reshape(bb, tt, A_WIDTH)
    v = proj(2 * A_WIDTH, 3 * A_WIDTH)
    if prompt:
        kt = lax.dot_general(r['wkt'][...], h, _NT, preferred_element_type=_F32)
        kt = _rope_pairs(kt, r['ct'][...], r['s1t'][...], r['s2t'][...], 0)
        r['kt'][0] = kt
        r['kbt'][0, 0] = kt.astype(_BF16)
        for j in range(A_HEADS):
            r['v4'][0, pl.ds(j, tt, stride=A_HEADS), :] = v[:, j * PAIR:(j + 1) * PAIR]
        r['vb'][...] = v.astype(_BF16).reshape(bb, tt, A_WIDTH)
    else:
        k = _rope_pairs(proj(A_WIDTH, 2 * A_WIDTH), c, s1, s2, 1)
        r['k'][...] = k.reshape(bb, tt, A_WIDTH)
        r['v'][...] = v.reshape(bb, tt, A_WIDTH)

    o = 3 * A_WIDTH
    u = _gelu_tanh(proj(o, o + B_WIDTH))
    vn = _rms(_gelu_tanh(proj(o + B_WIDTH, o + 2 * B_WIDTH)), r['gn'][...])
    if not prompt:
        r['vn'][...] = vn.reshape(bb, tt, B_WIDTH)
    vnb = vn.astype(_BF16)
    row = lax.broadcasted_iota(jnp.int32, (chunk, chunk), 0)
    col = lax.broadcasted_iota(jnp.int32, (chunk, chunk), 1)
    tri = row >= col
    ws = jnp.concatenate(
        [jnp.where(tri, r['ws'][hd], 0.0).astype(_BF16) for hd in range(B_HEADS)], axis=0)
    lane = lax.broadcasted_iota(jnp.int32, (chunk, B_WIDTH), 1)
    bias = r['bias'][...]
    mixed_parts = []
    for r0 in range(0, m, chunk):
        mm = jnp.dot(ws, vnb[r0:r0 + chunk], preferred_element_type=_F32)
        mixed = mm[0:chunk]
        for hd in range(1, B_HEADS):
            mixed = jnp.where(lane >= hd * B_HEAD_DIM, mm[hd * chunk:(hd + 1) * chunk], mixed)
        mixed_parts.append(mixed + bias)
    b_out = u * jnp.concatenate(mixed_parts, axis=0)

    r['bc'][...] = jnp.concatenate([b_out, c_out], axis=1).astype(_BF16).reshape(
        bb, tt, B_WIDTH + C_WIDTH)

    @pl.when(ti == pl.num_programs(1) - 1)
    def _():
        r['nc'][...] = buf_ref[:, tt + lo:tt + HIST_PAD, :]


def _front(x, tabs, lw, *, layer=0, depth=1, tabs_t=None, kv_all=None, cache_conv=None):
    bsz, t, _ = x.shape
    prompt = tabs_t is not None
    if prompt:
        bb, tt = 1, FRONT_ROWS
    else:
        bb, tt = FRONT_ROWS // t, t
    chunk = min(t, GMLP_CHUNK)
    m = bb * tt
    nt = t // tt
    full = lambda shape: pl.BlockSpec(shape, lambda b, i: (0,) * len(shape))
    act = lambda w: pl.BlockSpec((bb, tt, w), lambda b, i: (b, i, 0))
    tab = pl.BlockSpec((m, PAIR), lambda b, i: (i, 0))
    hist = pl.BlockSpec((bb, HIST, C_WIDTH), lambda b, i: (b, 0, 0))
    in_specs = [act(D_MODEL), tab, tab, tab, full((1, D_MODEL)), full((D_MODEL, D_IN)),
                full((1, B_WIDTH)), full((B_HEADS, chunk, chunk)), full((chunk, B_WIDTH)),
                full((CONV_WIDTH, C_WIDTH)), full((1, C_WIDTH)), full((1, C_WIDTH)),
                full((1, C_WIDTH))]
    args = [x, *tabs, lw['g_pre'], lw['w_in'], lw['gmlp_norm'], lw['w_s'], lw['bias'],
            lw['conv_w'], lw['conv_b'], lw['ln_g'], lw['ln_b']]
    sds = jax.ShapeDtypeStruct
    aliases = {}
    if prompt:
        tab_t = pl.BlockSpec((PAIR, tt), lambda b, i: (0, i))
        in_specs += [tab_t, tab_t, tab_t, full((A_WIDTH, D_MODEL))]
        args += [*tabs_t, lw['w_k_t']]
        if kv_all is not None:
            aliases = {len(args): 1, len(args) + 1: 2}
            in_specs += [pl.BlockSpec(memory_space=pl.ANY)] * 2
            args += list(kv_all)
        out_shape = [sds((bsz, t, A_WIDTH), _BF16),
                     sds((depth, bsz, A_WIDTH, t), _F32),
                     sds((depth, bsz, t * A_HEADS, PAIR), _F32),
                     sds((bsz, nt, A_WIDTH, tt), _BF16),
                     sds((bsz, t, A_WIDTH), _BF16), sds((bsz, t, A_WIDTH), _BF16),
                     sds((bsz, HIST, C_WIDTH), _F32)]
        out_specs = [act(A_WIDTH),
                     pl.BlockSpec((None, 1, A_WIDTH, tt), lambda b, i: (layer, b, 0, i)),
                     pl.BlockSpec((None, 1, tt * A_HEADS, PAIR), lambda b, i: (layer, b, i, 0)),
                     pl.BlockSpec((1, 1, A_WIDTH, tt), lambda b, i: (b, i, 0, 0)),
                     act(A_WIDTH), act(A_WIDTH), hist]
    else:
        in_specs.append(hist)
        args.append(cache_conv)
        out_shape = [sds((bsz, t, A_WIDTH), _BF16), sds((bsz, t, A_WIDTH), _F32),
                     sds((bsz, t, A_WIDTH), _F32), sds((bsz, t, B_WIDTH), _F32),
                     sds((bsz, t, A_WIDTH), _BF16), sds((bsz, HIST, C_WIDTH), _F32)]
        out_specs = [act(A_WIDTH)] * 3 + [act(B_WIDTH), act(A_WIDTH), hist]
    return pl.pallas_call(
        functools.partial(_front_kernel, bb=bb, tt=tt, chunk=chunk, prompt=prompt,
                          n_alias=len(aliases)),
        grid=(bsz // bb, nt),
        in_specs=in_specs,
        out_specs=out_specs,
        out_shape=out_shape,
        input_output_aliases=aliases,
        scratch_shapes=[pltpu.VMEM((bb, HIST_PAD + tt, C_WIDTH), _F32),
                        pltpu.VMEM((SUBLANES, bb, tt + HIST_PAD - SUBLANES, C_WIDTH), _F32)],
        compiler_params=pltpu.CompilerParams(
            dimension_semantics=("arbitrary", "arbitrary"), vmem_limit_bytes=VMEM_LIMIT_BYTES),
        name="front_prompt" if prompt else "front_sample",
    )(*args)


def _diff_lambda(lam_ref, lam_init):
    lq = lam_ref[...]
    a = jnp.sum(lq[0:1] * lq[1:2], axis=-1, keepdims=True)
    b = jnp.sum(lq[2:3] * lq[3:4], axis=-1, keepdims=True)
    return jnp.exp(a) - jnp.exp(b) + lam_init


def _stack_query(q, n):
    lane = lax.broadcasted_iota(jnp.int32, (n, PAIR), 1)
    zero = jnp.zeros_like(q)
    return jnp.concatenate([jnp.where(lane < A_HEAD_DIM, q, zero),
                            jnp.where(lane >= A_HEAD_DIM, q, zero)], axis=0)


def _finish_head(acc, n, lam, subln, lam_init):
    o = acc[:, 0:PAIR] / acc[:, PAIR:2 * PAIR]
    d = o[0:n] - lam * o[n:2 * n]
    return _rms(d, subln) * (1.0 - lam_init)


def _with_ones(v):
    return jnp.concatenate([v, jnp.ones_like(v)], axis=1)


def _online_rows(blocks, vt, m_prev, acc_prev):
    m_cur = jnp.max(functools.reduce(jnp.maximum, blocks), axis=-1, keepdims=True)
    if m_prev is None:
        m_new = jnp.broadcast_to(m_cur, blocks[0].shape)
    else:
        m_new = jnp.maximum(m_prev, m_cur)
    p = jnp.concatenate([jnp.exp2((blk - m_new).astype(_BF16)) for blk in blocks], axis=1)
    acc_new = jnp.dot(p, vt, preferred_element_type=_F32)
    if m_prev is not None:
        alpha = jnp.exp2(m_prev - m_new)
        acc_new = jnp.concatenate([alpha, alpha], axis=1) * acc_prev + acc_new
    return m_new, acc_new


def _attn_prompt_kernel(lam_ref, sub_ref, q_ref, k_ref, v_ref, o_ref, qm_ref, va_ref, m_ref,
                        acc_ref, *, tq, rows, lam_init):
    nt = k_ref.shape[1]
    va_ref[...] = _with_ones(v_ref[0])
    chunks = range(0, 2 * tq, rows)
    steps = [(qi, ki) for qi in range(nt) for ki in [qi] + list(range(qi))]

    def key_counts(qi, ki):
        return [(r & (tq - 1)) + rows if ki == qi else tq for r in chunks]

    def score_matmuls(qi, ki):
        if ki == qi:
            qm_ref[qi] = _stack_query(q_ref[0, qi * tq:(qi + 1) * tq, :], tq)
        kt = k_ref[0, ki]
        return [jnp.dot(qm_ref[qi, r:r + rows], kt[:, 0:n], preferred_element_type=_F32)
                for r, n in zip(chunks, key_counts(qi, ki))]

    def softmax_chains(qi, ki, scores):
        diagonal = ki == qi
        vt = va_ref[ki * tq:(ki + 1) * tq, :]
        for r, n, s in zip(chunks, key_counts(qi, ki), scores):
            sl = slice(r, r + rows)
            q0 = r & (tq - 1)
            blocks = []
            for c in range(0, n, PAIR):
                blk = s[:, c:c + PAIR]
                if diagonal and c + PAIR > q0:
                    row = lax.broadcasted_iota(jnp.int32, (rows, PAIR), 0) + q0
                    col = lax.broadcasted_iota(jnp.int32, (rows, PAIR), 1) + c
                    blk = jnp.where((col >> CHUNK_SHIFT) <= (row >> CHUNK_SHIFT), blk, NEG_INF)
                blocks.append(blk)
            prev = (None, None) if diagonal else (m_ref[qi, sl], acc_ref[qi, sl])
            m_ref[qi, sl], acc_ref[qi, sl] = _online_rows(blocks, vt[0:n], *prev)

    lam = _diff_lambda(lam_ref, lam_init)
    scores = score_matmuls(*steps[0])
    for i, (qi, ki) in enumerate(steps):
        cur = scores
        if i + 1 < len(steps):
            scores = score_matmuls(*steps[i + 1])
        softmax_chains(qi, ki, cur)
        if i + 1 == len(steps) or steps[i + 1][0] != qi:
            o_ref[0, qi * tq:(qi + 1) * tq, :] = _finish_head(
                acc_ref[qi], tq, lam, sub_ref[...], lam_init).astype(_BF16)


def _attn_prompt(q, kbt, vb, lam_qk, subln, lam_init):
    bsz, t, _ = q.shape
    tq = ATTN_TILE
    nt = t // tq
    spec = pl.BlockSpec((1, t, PAIR), lambda b, j: (b, 0, j))
    kspec = pl.BlockSpec((1, nt, PAIR, tq), lambda b, j: (b, 0, j, 0))
    return pl.pallas_call(
        functools.partial(_attn_prompt_kernel, tq=tq, rows=ATTN_ROWS, lam_init=lam_init),
        grid=(bsz, A_HEADS),
        in_specs=[pl.BlockSpec((4, A_HEAD_DIM), lambda b, j: (0, 0)),
                  pl.BlockSpec((1, PAIR), lambda b, j: (0, 0)),
                  spec, kspec, spec],
        out_specs=spec,
        out_shape=jax.ShapeDtypeStruct((bsz, t, A_WIDTH), _BF16),
        scratch_shapes=[pltpu.VMEM((nt, 2 * tq, PAIR), _BF16), pltpu.VMEM((t, 2 * PAIR), _BF16),
                        pltpu.VMEM((nt, 2 * tq, PAIR), _F32),
                        pltpu.VMEM((nt, 2 * tq, 2 * PAIR), _F32)],
        compiler_params=pltpu.CompilerParams(
            dimension_semantics=("arbitrary",) * 2, vmem_limit_bytes=VMEM_LIMIT_BYTES),
        name="attn_prompt",
    )(lam_qk, subln, q, kbt, vb)


def _attn_sample_kernel(lam_ref, sub_ref, q_ref, kn_ref, vn_ref, kc_ref, vc_ref, o_ref,
                        qm_ref, m_ref, acc_ref, *, t, lam_init):
    ki = pl.program_id(1)
    tk = kc_ref.shape[2]

    @pl.when(ki == 0)
    def _():
        q = q_ref[0]
        pad = jnp.zeros((PAIR - t, PAIR), _BF16)
        col = lax.broadcasted_iota(jnp.int32, (2 * t, PAIR), 1)
        for j in range(A_HEADS):
            sl = slice(j * PAIR, (j + 1) * PAIR)
            qm_ref[j] = _stack_query(q[:, sl], t)
            kj = jnp.concatenate([kn_ref[0, :, sl].astype(_BF16), pad], axis=0)
            vj = jnp.concatenate([vn_ref[0, :, sl].astype(_BF16), pad], axis=0)
            s = lax.dot_general(qm_ref[j], kj, _NT, preferred_element_type=_F32)
            s = jnp.where(col < t, s, NEG_INF)
            m_ref[j], acc_ref[j] = _online_rows([s], _with_ones(vj), None, None)

    for j in range(A_HEADS):
        kj = kc_ref[2 * j:2 * j + 2].reshape(PAIR, tk).astype(_BF16)
        vj = vc_ref[pl.ds(j, tk, stride=A_HEADS), :].astype(_BF16)
        s = jnp.dot(qm_ref[j], kj, preferred_element_type=_F32)
        blocks = [s[:, c:c + PAIR] for c in range(0, tk, PAIR)]
        m_ref[j], acc_ref[j] = _online_rows(blocks, _with_ones(vj), m_ref[j], acc_ref[j])

    @pl.when(ki == pl.num_programs(1) - 1)
    def _():
        lam = _diff_lambda(lam_ref, lam_init)
        for j in range(A_HEADS):
            o_ref[0, :, j * PAIR:(j + 1) * PAIR] = _finish_head(
                acc_ref[j], t, lam, sub_ref[...], lam_init).astype(_BF16)


def _attn_sample(q, k_new, v_new, cache_kt, cache_v4, layer, lam_qk, subln, lam_init):
    bsz, t, _ = q.shape
    past = cache_kt.shape[-1]
    tk = CACHE_TILE
    new = pl.BlockSpec((1, t, A_WIDTH), lambda b, ki: (b, 0, 0))
    kspec = pl.BlockSpec((None, None, 2 * A_HEADS, A_HEAD_DIM, tk),
                         lambda b, ki: (layer, b, 0, 0, ki))
    vspec = pl.BlockSpec((None, None, tk * A_HEADS, PAIR), lambda b, ki: (layer, b, ki, 0))
    return pl.pallas_call(
        functools.partial(_attn_sample_kernel, t=t, lam_init=lam_init),
        grid=(bsz, past // tk),
        in_specs=[pl.BlockSpec((4, A_HEAD_DIM), lambda b, ki: (0, 0)),
                  pl.BlockSpec((1, PAIR), lambda b, ki: (0, 0)),
                  new, new, new, kspec, vspec],
        out_specs=new,
        out_shape=jax.ShapeDtypeStruct((bsz, t, A_WIDTH), _BF16),
        scratch_shapes=[pltpu.VMEM((A_HEADS, 2 * t, PAIR), _BF16),
                        pltpu.VMEM((A_HEADS, 2 * t, PAIR), _F32),
                        pltpu.VMEM((A_HEADS, 2 * t, 2 * PAIR), _F32)],
        compiler_params=pltpu.CompilerParams(
            dimension_semantics=("arbitrary", "arbitrary"), vmem_limit_bytes=VMEM_LIMIT_BYTES),
        name="attn_sample",
    )(lam_qk, subln, q, k_new, v_new, cache_kt, cache_v4)


def _back_kernel(x_ref, a_ref, bc_ref, wout_ref, gpost_ref, gpre_ref, wup_ref, wdown_ref,
                 gmlp_ref, o_ref):
    mix = jnp.dot(a_ref[...], wout_ref[0:A_WIDTH, :], preferred_element_type=_F32)
    mix = mix + jnp.dot(bc_ref[...], wout_ref[A_WIDTH:, :], preferred_element_type=_F32)
    x1 = x_ref[...] + _rms(mix, gpost_ref[...])
    h = _rms(x1, gpre_ref[...]).astype(_BF16)
    f = jnp.zeros(x1.shape, _F32)
    for c0 in range(0, D_FF, FF_CHUNK):
        up = jnp.dot(h, wup_ref[:, c0:c0 + FF_CHUNK], preferred_element_type=_F32)
        up = jnp.maximum(up, 0.0)
        f = f + jnp.dot((up * up).astype(_BF16), wdown_ref[c0:c0 + FF_CHUNK, :],
                        preferred_element_type=_F32)
    o_ref[...] = x1 + _rms(f, gmlp_ref[...])


def _back(x, a, bc, lw):
    m = x.shape[0]
    tm = BACK_ROWS
    full = lambda shape: pl.BlockSpec(shape, lambda i: (0, 0))
    rows = lambda w: pl.BlockSpec((tm, w), lambda i: (i, 0))
    return pl.pallas_call(
        _back_kernel,
        grid=(m // tm,),
        in_specs=[rows(D_MODEL), rows(A_WIDTH), rows(A_WIDTH), full((D_MODEL, D_MODEL)),
                  full((1, D_MODEL)), full((1, D_MODEL)), full((D_MODEL, D_FF)),
                  full((D_FF, D_MODEL)), full((1, D_MODEL))],
        out_specs=rows(D_MODEL),
        out_shape=jax.ShapeDtypeStruct((m, D_MODEL), _F32),
        compiler_params=pltpu.CompilerParams(
            dimension_semantics=("arbitrary",), vmem_limit_bytes=VMEM_LIMIT_BYTES),
        name="back",
    )(x, a, bc, lw['w_out'], lw['g_post'], lw['g_mlp_pre'], lw['w_up'], lw['w_down'],
      lw['g_mlp_post'])


def _rope_tables(pos0, t, reps):
    half = ROT_DIM // 2
    freqs = ROPE_THETA ** (-jnp.arange(0, ROT_DIM, 2, dtype=_F32) / ROT_DIM)
    pos = (pos0 + jnp.arange(t)).astype(_F32)
    ang = pos[:, None] * freqs[None, :]
    cos, sin = jnp.cos(ang), jnp.sin(ang)
    pad = A_HEAD_DIM - ROT_DIM
    one = jnp.ones((t, pad), _F32)
    zero = jnp.zeros((t, pad), _F32)
    zh = jnp.zeros((t, half), _F32)
    c = jnp.concatenate([cos, cos, one], axis=1)
    s1 = jnp.concatenate([zh, sin, zero], axis=1)
    s2 = jnp.concatenate([-sin, zh, zero], axis=1)
    return tuple(jnp.tile(jnp.tile(a, (1, 2)), (reps, 1)) for a in (c, s1, s2))


def kernel(x_prompt, x_sample, cache_k, cache_v, cache_conv, norm_mix_pre, norm_mix_post,
           norm_mlp_pre, norm_mlp_post, w_in, diff_lambda, diff_subln, gmlp_norm, gmlp_w_s,
           gmlp_bias, conv_w, conv_b, conv_ln_gain, conv_ln_bias, w_out, w_up, w_down):
    depth = w_in.shape[0]
    bp, tp, _ = x_prompt.shape
    bs, ts, _ = x_sample.shape
    past = cache_k.shape[2]
    cache_kt = jnp.transpose(cache_k, (0, 1, 3, 4, 2))
    cache_v4 = cache_v.reshape(depth, bs, past * A_HEADS, PAIR)
    tabs_p = _rope_tables(0, tp, 1)
    tabs_pt = tuple(a.T for a in tabs_p)
    tabs_s = _rope_tables(past, ts, FRONT_ROWS // ts)
    w_in_b, w_out_b = w_in.astype(_BF16), w_out.astype(_BF16)
    w_up_b, w_down_b = w_up.astype(_BF16), w_down.astype(_BF16)

    def layer_weights(l, chunk):
        return dict(
            g_pre=norm_mix_pre[l][None], g_post=norm_mix_post[l][None],
            g_mlp_pre=norm_mlp_pre[l][None], g_mlp_post=norm_mlp_post[l][None],
            w_in=w_in_b[l], w_k_t=w_in_b[l][:, A_WIDTH:2 * A_WIDTH].T,
            lam_qk=diff_lambda[l], subln=diff_subln[l][None],
            gmlp_norm=gmlp_norm[l][None], w_s=gmlp_w_s[l][:, :chunk, :chunk],
            bias=jnp.repeat(gmlp_bias[l][:, :chunk].T, B_HEAD_DIM, axis=1),
            conv_w=conv_w[l], conv_b=conv_b[l][None], ln_g=conv_ln_gain[l][None],
            ln_b=conv_ln_bias[l][None], w_out=w_out_b[l], w_up=w_up_b[l], w_down=w_down_b[l])

    xp, xs = x_prompt, x_sample
    kv_all = None
    outs = [[] for _ in range(5)]
    for l in range(depth):
        lam_init = 0.8 - 0.6 * math.exp(-0.3 * l)

        lw = layer_weights(l, min(tp, GMLP_CHUNK))
        q, kt_all, v4_all, kbt, vb, bc, c_p = _front(xp, tabs_p, lw, layer=l, depth=depth,
                                                   tabs_t=tabs_pt, kv_all=kv_all)
        kv_all = (kt_all, v4_all)
        a = _attn_prompt(q, kbt, vb, lw['lam_qk'], lw['subln'], lam_init)
        xp = _back(xp.reshape(bp * tp, D_MODEL), a.reshape(bp * tp, A_WIDTH),
                   bc.reshape(bp * tp, A_WIDTH), lw).reshape(bp, tp, D_MODEL)

        lw = layer_weights(l, min(ts, GMLP_CHUNK))
        q, k_s, v_s, g_s, bc, c_s = _front(xs, tabs_s, lw, cache_conv=cache_conv[l])
        a = _attn_sample(q, k_s, v_s, cache_kt, cache_v4, l, lw['lam_qk'], lw['subln'], lam_init)
        xs = _back(xs.reshape(bs * ts, D_MODEL), a.reshape(bs * ts, A_WIDTH),
                   bc.reshape(bs * ts, A_WIDTH), lw).reshape(bs, ts, D_MODEL)

        for lst, val in zip(outs, (c_p, k_s, v_s, c_s, g_s)):
            lst.append(val)
    cp, ksl, vsl, csl, gsl = (jnp.stack(o) for o in outs)
    kt_all, v4_all = kv_all
    kp = jnp.transpose(kt_all.reshape(depth, bp, 2 * A_HEADS, A_HEAD_DIM, tp), (0, 1, 4, 2, 3))
    return (xp, xs, kp,
            v4_all.reshape(depth, bp, tp, A_HEADS, 2 * A_HEAD_DIM),
            cp,
            ksl.reshape(depth, bs, ts, 2 * A_HEADS, A_HEAD_DIM),
            vsl.reshape(depth, bs, ts, A_HEADS, 2 * A_HEAD_DIM),
            csl,
            gsl.reshape(depth, bs, ts, B_HEADS, B_HEAD_DIM))
```

```python
import functools
import math

import jax
import jax.numpy as jnp
from jax import lax
from jax.experimental import pallas as pl
from jax.experimental.pallas import tpu as pltpu

D_MODEL = 1024
A_HEADS = 4
A_HEAD_DIM = 64
A_WIDTH = A_HEADS * 2 * A_HEAD_DIM
PAIR = 2 * A_HEAD_DIM
ROT_DIM = A_HEAD_DIM // 4
ROPE_THETA = 500000.0
B_HEADS = 4
B_HEAD_DIM = 64
B_WIDTH = B_HEADS * B_HEAD_DIM
GMLP_CHUNK = 128
C_WIDTH = 256
CONV_WIDTH = 31
HIST = CONV_WIDTH - 1
SUBLANES = 8
HIST_PAD = 32
D_IN = 3 * A_WIDTH + 2 * B_WIDTH + 2 * C_WIDTH
D_FF = 4 * D_MODEL
EPS = 1e-6
NEG_INF = -1e30
LOG2_E = math.log2(math.e)
CHUNK = 64
CHUNK_SHIFT = CHUNK.bit_length() - 1

VMEM_LIMIT_BYTES = 56 * 1024 * 1024
ATTN_TILE = 512
FRONT_ROWS = ATTN_TILE
ATTN_ROWS = 128
CACHE_TILE = 2048
BACK_ROWS = 512
FF_CHUNK = 1024

_BF16 = jnp.bfloat16
_F32 = jnp.float32
_NT = (((1,), (1,)), ((), ()))


def _rms(x, g):
    return x * lax.rsqrt(jnp.mean(x * x, axis=-1, keepdims=True) + EPS) * g


def _gelu_tanh(x):
    c = math.sqrt(2.0 / math.pi)
    return 0.5 * x * (1.0 + jnp.tanh(c * (x + 0.044715 * (x * x * x))))


def _sigmoid(x):
    return 1.0 / (1.0 + jnp.exp(-x))


def _rope_pairs(z, c, s1, s2, axis):
    outs = []
    for j in range(A_WIDTH // PAIR):
        zj = z[:, j * PAIR:(j + 1) * PAIR] if axis == 1 else z[j * PAIR:(j + 1) * PAIR, :]
        up = pltpu.roll(zj, ROT_DIM // 2, axis=axis)
        dn = pltpu.roll(zj, PAIR - ROT_DIM // 2, axis=axis)
        outs.append(zj * c + up * s1 + dn * s2)
    return jnp.concatenate(outs, axis=axis)


_FRONT_COMMON = ('x', 'c', 's1', 's2', 'gpre', 'win', 'gn', 'ws', 'bias', 'cw', 'cb', 'lg', 'lb')


def _front_kernel(*refs, bb, tt, chunk, prompt, n_alias):
    if prompt:
        names = _FRONT_COMMON + ('ct', 's1t', 's2t', 'wkt') + ('alias',) * n_alias + (
            'q', 'kt', 'v4', 'kbt', 'vb', 'bc', 'nc', 'buf', 'xs')
    else:
        names = _FRONT_COMMON + ('cache', 'q', 'k', 'v', 'vn', 'bc', 'nc', 'buf', 'xs')
    r = dict(zip(names, refs))
    ti = pl.program_id(1)
    m = bb * tt

    x = r['x'][...].reshape(m, D_MODEL)
    h = _rms(x, r['gpre'][...]).astype(_BF16)
    win_ref = r['win']

    def proj(lo, hi):
        return jnp.dot(h, win_ref[:, lo:hi], preferred_element_type=_F32)

    o = 3 * A_WIDTH + 2 * B_WIDTH
    glu = proj(o, o + C_WIDTH) * _sigmoid(proj(o + C_WIDTH, o + 2 * C_WIDTH))
    glu = glu.reshape(bb, tt, C_WIDTH)
    buf_ref, xs_ref = r['buf'], r['xs']
    lo = HIST_PAD - HIST
    if prompt:
        @pl.when(ti == 0)
        def _():
            buf_ref[:, 0:HIST_PAD, :] = jnp.zeros((bb, HIST_PAD, C_WIDTH), _F32)

        @pl.when(ti > 0)
        def _():
            buf_ref[:, 0:HIST_PAD, :] = buf_ref[:, tt:tt + HIST_PAD, :]
    else:
        buf_ref[:, 0:lo, :] = jnp.zeros((bb, lo, C_WIDTH), _F32)
        buf_ref[:, lo:HIST_PAD, :] = r['cache'][...]
    buf_ref[:, HIST_PAD:HIST_PAD + tt, :] = glu
    for p in range(SUBLANES):
        n = tt + SUBLANES * ((CONV_WIDTH - 1 - p) // SUBLANES)
        xs_ref[p, :, 0:n, :] = buf_ref[:, lo + p:lo + p + n, :]
    cw = r['cw'][...]
    y = jnp.zeros((bb, tt, C_WIDTH), _F32) + r['cb'][...]
    for j in range(CONV_WIDTH):
        a, p = divmod(j, SUBLANES)
        y = y + cw[j:j + 1, :] * xs_ref[p, :, SUBLANES * a:SUBLANES * a + tt, :]
    mu = jnp.mean(y, axis=-1, keepdims=True)
    yc = y - mu
    var = jnp.mean(yc * yc, axis=-1, keepdims=True)
    yn = yc * lax.rsqrt(var + EPS) * r['lg'][...] + r['lb'][...]
    c_out = (yn * _sigmoid(yn)).reshape(m, C_WIDTH)

    c, s1, s2 = r['c'][...], r['s1'][...], r['s2'][...]
    q = _rope_pairs(proj(0, A_WIDTH), c, s1, s2, 1) * (A_HEAD_DIM ** -0.5 * LOG2_E)
    r['q'][...] = q.astype(_BF16).reshape(bb, tt, A_WIDTH)
    v = proj(2 * A_WIDTH, 3 * A_WIDTH)
    if prompt:
        kt = lax.dot_general(r['wkt'][...], h, _NT, preferred_element_type=_F32)
        kt = _rope_pairs(kt, r['ct'][...], r['s1t'][...], r['s2t'][...], 0)
        r['kt'][0] = kt
        r['kbt'][0, 0] = kt.astype(_BF16)
        for j in range(A_HEADS):
            r['v4'][0, pl.ds(j, tt, stride=A_HEADS), :] = v[:, j * PAIR:(j + 1) * PAIR]
        r['vb'][...] = v.astype(_BF16).reshape(bb, tt, A_WIDTH)
    else:
        k = _rope_pairs(proj(A_WIDTH, 2 * A_WIDTH), c, s1, s2, 1)
        r['k'][...] = k.reshape(bb, tt, A_WIDTH)
        r['v'][...] = v.reshape(bb, tt, A_WIDTH)

    o = 3 * A_WIDTH
    u = _gelu_tanh(proj(o, o + B_WIDTH))
    vn = _rms(_gelu_tanh(proj(o + B_WIDTH, o + 2 * B_WIDTH)), r['gn'][...])
    if not prompt:
        r['vn'][...] = vn.reshape(bb, tt, B_WIDTH)
    vnb = vn.astype(_BF16)
    row = lax.broadcasted_iota(jnp.int32, (chunk, chunk), 0)
    col = lax.broadcasted_iota(jnp.int32, (chunk, chunk), 1)
    tri = row >= col
    ws = jnp.concatenate(
        [jnp.where(tri, r['ws'][hd], 0.0).astype(_BF16) for hd in range(B_HEADS)], axis=0)
    lane = lax.broadcasted_iota(jnp.int32, (chunk, B_WIDTH), 1)
    bias = r['bias'][...]
    mixed_parts = []
    for r0 in range(0, m, chunk):
        mm = jnp.dot(ws, vnb[r0:r0 + chunk], preferred_element_type=_F32)
        mixed = mm[0:chunk]
        for hd in range(1, B_HEADS):
            mixed = jnp.where(lane >= hd * B_HEAD_DIM, mm[hd * chunk:(hd + 1) * chunk], mixed)
        mixed_parts.append(mixed + bias)
    b_out = u * jnp.concatenate(mixed_parts, axis=0)

    r['bc'][...] = jnp.concatenate([b_out, c_out], axis=1).astype(_BF16).reshape(
        bb, tt, B_WIDTH + C_WIDTH)

    @pl.when(ti == pl.num_programs(1) - 1)
    def _():
        r['nc'][...] = buf_ref[:, tt + lo:tt + HIST_PAD, :]


def _front(x, tabs, lw, *, layer=0, depth=1, tabs_t=None, kv_all=None, cache_conv=None):
    bsz, t, _ = x.shape
    prompt = tabs_t is not None
    if prompt:
        bb, tt = 1, FRONT_ROWS
    else:
        bb, tt = FRONT_ROWS // t, t
    chunk = min(t, GMLP_CHUNK)
    m = bb * tt
    nt = t // tt
    full = lambda shape: pl.BlockSpec(shape, lambda b, i: (0,) * len(shape))
    wmat = lambda shape: pl.BlockSpec((None,) + shape, lambda b, i: (lw['layer'], 0, 0))
    act = lambda w: pl.BlockSpec((bb, tt, w), lambda b, i: (b, i, 0))
    tab = pl.BlockSpec((m, PAIR), lambda b, i: (i, 0))
    hist = pl.BlockSpec((bb, HIST, C_WIDTH), lambda b, i: (b, 0, 0))
    in_specs = [act(D_MODEL), tab, tab, tab, full((1, D_MODEL)), wmat((D_MODEL, D_IN)),
                full((1, B_WIDTH)), full((B_HEADS, chunk, chunk)), full((chunk, B_WIDTH)),
                full((CONV_WIDTH, C_WIDTH)), full((1, C_WIDTH)), full((1, C_WIDTH)),
                full((1, C_WIDTH))]
    args = [x, *tabs, lw['g_pre'], lw['w_in'], lw['gmlp_norm'], lw['w_s'], lw['bias'],
            lw['conv_w'], lw['conv_b'], lw['ln_g'], lw['ln_b']]
    sds = jax.ShapeDtypeStruct
    aliases = {}
    if prompt:
        tab_t = pl.BlockSpec((PAIR, tt), lambda b, i: (0, i))
        in_specs += [tab_t, tab_t, tab_t, wmat((A_WIDTH, D_MODEL))]
        args += [*tabs_t, lw['w_k_t']]
        if kv_all is not None:
            aliases = {len(args): 1, len(args) + 1: 2}
            in_specs += [pl.BlockSpec(memory_space=pl.ANY)] * 2
            args += list(kv_all)
        out_shape = [sds((bsz, t, A_WIDTH), _BF16),
                     sds((depth, bsz, A_WIDTH, t), _F32),
                     sds((depth, bsz, t * A_HEADS, PAIR), _F32),
                     sds((bsz, nt, A_WIDTH, tt), _BF16),
                     sds((bsz, t, A_WIDTH), _BF16), sds((bsz, t, A_WIDTH), _BF16),
                     sds((bsz, HIST, C_WIDTH), _F32)]
        out_specs = [act(A_WIDTH),
                     pl.BlockSpec((None, 1, A_WIDTH, tt), lambda b, i: (layer, b, 0, i)),
                     pl.BlockSpec((None, 1, tt * A_HEADS, PAIR), lambda b, i: (layer, b, i, 0)),
                     pl.BlockSpec((1, 1, A_WIDTH, tt), lambda b, i: (b, i, 0, 0)),
                     act(A_WIDTH), act(A_WIDTH), hist]
    else:
        in_specs.append(hist)
        args.append(cache_conv)
        out_shape = [sds((bsz, t, A_WIDTH), _BF16), sds((bsz, t, A_WIDTH), _F32),
                     sds((bsz, t, A_WIDTH), _F32), sds((bsz, t, B_WIDTH), _F32),
                     sds((bsz, t, A_WIDTH), _BF16), sds((bsz, HIST, C_WIDTH), _F32)]
        out_specs = [act(A_WIDTH)] * 3 + [act(B_WIDTH), act(A_WIDTH), hist]
    return pl.pallas_call(
        functools.partial(_front_kernel, bb=bb, tt=tt, chunk=chunk, prompt=prompt,
                          n_alias=len(aliases)),
        grid=(bsz // bb, nt),
        in_specs=in_specs,
        out_specs=out_specs,
        out_shape=out_shape,
        input_output_aliases=aliases,
        scratch_shapes=[pltpu.VMEM((bb, HIST_PAD + tt, C_WIDTH), _F32),
                        pltpu.VMEM((SUBLANES, bb, tt + HIST_PAD - SUBLANES, C_WIDTH), _F32)],
        compiler_params=pltpu.CompilerParams(
            dimension_semantics=("arbitrary", "arbitrary"), vmem_limit_bytes=VMEM_LIMIT_BYTES),
        name="front_prompt" if prompt else "front_sample",
    )(*args)


def _diff_lambda(lam_ref, lam_init):
    lq = lam_ref[...]
    a = jnp.sum(lq[0:1] * lq[1:2], axis=-1, keepdims=True)
    b = jnp.sum(lq[2:3] * lq[3:4], axis=-1, keepdims=True)
    return jnp.exp(a) - jnp.exp(b) + lam_init


def _stack_query(q, n):
    lane = lax.broadcasted_iota(jnp.int32, (n, PAIR), 1)
    zero = jnp.zeros_like(q)
    return jnp.concatenate([jnp.where(lane < A_HEAD_DIM, q, zero),
                            jnp.where(lane >= A_HEAD_DIM, q, zero)], axis=0)


def _finish_head(acc, n, lam, subln, lam_init):
    o = acc[:, 0:PAIR] / acc[:, PAIR:2 * PAIR]
    d = o[0:n] - lam * o[n:2 * n]
    return _rms(d, subln) * (1.0 - lam_init)


def _with_ones(v):
    return jnp.concatenate([v, jnp.ones_like(v)], axis=1)


def _online_rows(blocks, vt, m_prev, acc_prev):
    m_cur = jnp.max(functools.reduce(jnp.maximum, blocks), axis=-1, keepdims=True)
    if m_prev is None:
        m_new = jnp.broadcast_to(m_cur, blocks[0].shape)
    else:
        m_new = jnp.maximum(m_prev, m_cur)
    p = jnp.concatenate([jnp.exp2((blk - m_new).astype(_BF16)) for blk in blocks], axis=1)
    acc_new = jnp.dot(p, vt, preferred_element_type=_F32)
    if m_prev is not None:
        alpha = jnp.exp2(m_prev - m_new)
        acc_new = jnp.concatenate([alpha, alpha], axis=1) * acc_prev + acc_new
    return m_new, acc_new


def _attn_prompt_kernel(lam_ref, sub_ref, q_ref, k_ref, v_ref, o_ref, qm_ref, va_ref, m_ref,
                        acc_ref, *, tq, rows, lam_init):
    nt = k_ref.shape[1]
    va_ref[...] = _with_ones(v_ref[0])
    chunks = range(0, 2 * tq, rows)
    steps = [(qi, ki) for qi in range(nt) for ki in [qi] + list(range(qi))]

    def key_counts(qi, ki):
        return [(r & (tq - 1)) + rows if ki == qi else tq for r in chunks]

    def score_matmuls(qi, ki):
        if ki == qi:
            qm_ref[qi] = _stack_query(q_ref[0, qi * tq:(qi + 1) * tq, :], tq)
        kt = k_ref[0, ki]
        return [jnp.dot(qm_ref[qi, r:r + rows], kt[:, 0:n], preferred_element_type=_F32)
                for r, n in zip(chunks, key_counts(qi, ki))]

    def softmax_chains(qi, ki, scores):
        diagonal = ki == qi
        vt = va_ref[ki * tq:(ki + 1) * tq, :]
        for r, n, s in zip(chunks, key_counts(qi, ki), scores):
            sl = slice(r, r + rows)
            q0 = r & (tq - 1)
            blocks = []
            for c in range(0, n, PAIR):
                blk = s[:, c:c + PAIR]
                if diagonal and c + PAIR > q0:
                    row = lax.broadcasted_iota(jnp.int32, (rows, PAIR), 0) + q0
                    col = lax.broadcasted_iota(jnp.int32, (rows, PAIR), 1) + c
                    blk = jnp.where((col >> CHUNK_SHIFT) <= (row >> CHUNK_SHIFT), blk, NEG_INF)
                blocks.append(blk)
            prev = (None, None) if diagonal else (m_ref[qi, sl], acc_ref[qi, sl])
            m_ref[qi, sl], acc_ref[qi, sl] = _online_rows(blocks, vt[0:n], *prev)

    lam = _diff_lambda(lam_ref, lam_init)
    scores = score_matmuls(*steps[0])
    for i, (qi, ki) in enumerate(steps):
        cur = scores
        if i + 1 < len(steps):
            scores = score_matmuls(*steps[i + 1])
        softmax_chains(qi, ki, cur)
        if i + 1 == len(steps) or steps[i + 1][0] != qi:
            o_ref[0, qi * tq:(qi + 1) * tq, :] = _finish_head(
                acc_ref[qi], tq, lam, sub_ref[...], lam_init).astype(_BF16)


def _attn_prompt(q, kbt, vb, lam_qk, subln, lam_init):
    bsz, t, _ = q.shape
    tq = ATTN_TILE
    nt = t // tq
    spec = pl.BlockSpec((1, t, PAIR), lambda b, j: (b, 0, j))
    kspec = pl.BlockSpec((1, nt, PAIR, tq), lambda b, j: (b, 0, j, 0))
    return pl.pallas_call(
        functools.partial(_attn_prompt_kernel, tq=tq, rows=ATTN_ROWS, lam_init=lam_init),
        grid=(bsz, A_HEADS),
        in_specs=[pl.BlockSpec((4, A_HEAD_DIM), lambda b, j: (0, 0)),
                  pl.BlockSpec((1, PAIR), lambda b, j: (0, 0)),
                  spec, kspec, spec],
        out_specs=spec,
        out_shape=jax.ShapeDtypeStruct((bsz, t, A_WIDTH), _BF16),
        scratch_shapes=[pltpu.VMEM((nt, 2 * tq, PAIR), _BF16), pltpu.VMEM((t, 2 * PAIR), _BF16),
                        pltpu.VMEM((nt, 2 * tq, PAIR), _F32),
                        pltpu.VMEM((nt, 2 * tq, 2 * PAIR), _F32)],
        compiler_params=pltpu.CompilerParams(
            dimension_semantics=("arbitrary",) * 2, vmem_limit_bytes=VMEM_LIMIT_BYTES),
        name="attn_prompt",
    )(lam_qk, subln, q, kbt, vb)


def _attn_sample_kernel(lam_ref, sub_ref, q_ref, kn_ref, vn_ref, kc_ref, vc_ref, o_ref,
                        qm_ref, m_ref, acc_ref, *, t, lam_init):
    ki = pl.program_id(1)
    tk = kc_ref.shape[2]

    @pl.when(ki == 0)
    def _():
        q = q_ref[0]
        pad = jnp.zeros((PAIR - t, PAIR), _BF16)
        col = lax.broadcasted_iota(jnp.int32, (2 * t, PAIR), 1)
        for j in range(A_HEADS):
            sl = slice(j * PAIR, (j + 1) * PAIR)
            qm_ref[j] = _stack_query(q[:, sl], t)
            kj = jnp.concatenate([kn_ref[0, :, sl].astype(_BF16), pad], axis=0)
            vj = jnp.concatenate([vn_ref[0, :, sl].astype(_BF16), pad], axis=0)
            s = lax.dot_general(qm_ref[j], kj, _NT, preferred_element_type=_F32)
            s = jnp.where(col < t, s, NEG_INF)
            m_ref[j], acc_ref[j] = _online_rows([s], _with_ones(vj), None, None)

    for j in range(A_HEADS):
        kj = kc_ref[2 * j:2 * j + 2].reshape(PAIR, tk).astype(_BF16)
        vj = vc_ref[pl.ds(j, tk, stride=A_HEADS), :].astype(_BF16)
        s = jnp.dot(qm_ref[j], kj, preferred_element_type=_F32)
        blocks = [s[:, c:c + PAIR] for c in range(0, tk, PAIR)]
        m_ref[j], acc_ref[j] = _online_rows(blocks, _with_ones(vj), m_ref[j], acc_ref[j])

    @pl.when(ki == pl.num_programs(1) - 1)
    def _():
        lam = _diff_lambda(lam_ref, lam_init)
        for j in range(A_HEADS):
            o_ref[0, :, j * PAIR:(j + 1) * PAIR] = _finish_head(
                acc_ref[j], t, lam, sub_ref[...], lam_init).astype(_BF16)


def _attn_sample(q, k_new, v_new, cache_kt, cache_v4, layer, lam_qk, subln, lam_init):
    bsz, t, _ = q.shape
    past = cache_kt.shape[-1]
    tk = CACHE_TILE
    new = pl.BlockSpec((1, t, A_WIDTH), lambda b, ki: (b, 0, 0))
    kspec = pl.BlockSpec((None, None, 2 * A_HEADS, A_HEAD_DIM, tk),
                         lambda b, ki: (layer, b, 0, 0, ki))
    vspec = pl.BlockSpec((None, None, tk * A_HEADS, PAIR), lambda b, ki: (layer, b, ki, 0))
    return pl.pallas_call(
        functools.partial(_attn_sample_kernel, t=t, lam_init=lam_init),
        grid=(bsz, past // tk),
        in_specs=[pl.BlockSpec((4, A_HEAD_DIM), lambda b, ki: (0, 0)),
                  pl.BlockSpec((1, PAIR), lambda b, ki: (0, 0)),
                  new, new, new, kspec, vspec],
        out_specs=new,
        out_shape=jax.ShapeDtypeStruct((bsz, t, A_WIDTH), _BF16),
        scratch_shapes=[pltpu.VMEM((A_HEADS, 2 * t, PAIR), _BF16),
                        pltpu.VMEM((A_HEADS, 2 * t, PAIR), _F32),
                        pltpu.VMEM((A_HEADS, 2 * t, 2 * PAIR), _F32)],
        compiler_params=pltpu.CompilerParams(
            dimension_semantics=("arbitrary", "arbitrary"), vmem_limit_bytes=VMEM_LIMIT_BYTES),
        name="attn_sample",
    )(lam_qk, subln, q, k_new, v_new, cache_kt, cache_v4)


def _back_kernel(x_ref, a_ref, bc_ref, wout_ref, gpost_ref, gpre_ref, wup_ref, wdown_ref,
                 gmlp_ref, o_ref):
    mix = jnp.dot(a_ref[...], wout_ref[0:A_WIDTH, :], preferred_element_type=_F32)
    mix = mix + jnp.dot(bc_ref[...], wout_ref[A_WIDTH:, :], preferred_element_type=_F32)
    x1 = x_ref[...] + _rms(mix, gpost_ref[...])
    h = _rms(x1, gpre_ref[...]).astype(_BF16)
    f = jnp.zeros(x1.shape, _F32)
    for c0 in range(0, D_FF, FF_CHUNK):
        up = jnp.dot(h, wup_ref[:, c0:c0 + FF_CHUNK], preferred_element_type=_F32)
        up = jnp.maximum(up, 0.0)
        f = f + jnp.dot((up * up).astype(_BF16), wdown_ref[c0:c0 + FF_CHUNK, :],
                        preferred_element_type=_F32)
    o_ref[...] = x1 + _rms(f, gmlp_ref[...])


def _back(x, a, bc, lw):
    m = x.shape[0]
    tm = BACK_ROWS
    full = lambda shape: pl.BlockSpec(shape, lambda i: (0, 0))
    wmat = lambda shape: pl.BlockSpec((None,) + shape, lambda i: (lw['layer'], 0, 0))
    rows = lambda w: pl.BlockSpec((tm, w), lambda i: (i, 0))
    return pl.pallas_call(
        _back_kernel,
        grid=(m // tm,),
        in_specs=[rows(D_MODEL), rows(A_WIDTH), rows(A_WIDTH), wmat((D_MODEL, D_MODEL)),
                  full((1, D_MODEL)), full((1, D_MODEL)), wmat((D_MODEL, D_FF)),
                  wmat((D_FF, D_MODEL)), full((1, D_MODEL))],
        out_specs=rows(D_MODEL),
        out_shape=jax.ShapeDtypeStruct((m, D_MODEL), _F32),
        compiler_params=pltpu.CompilerParams(
            dimension_semantics=("arbitrary",), vmem_limit_bytes=VMEM_LIMIT_BYTES),
        name="back",
    )(x, a, bc, lw['w_out'], lw['g_post'], lw['g_mlp_pre'], lw['w_up'], lw['w_down'],
      lw['g_mlp_post'])


def _rope_tables(pos0, t, reps):
    half = ROT_DIM // 2
    freqs = ROPE_THETA ** (-jnp.arange(0, ROT_DIM, 2, dtype=_F32) / ROT_DIM)
    pos = (pos0 + jnp.arange(t)).astype(_F32)
    ang = pos[:, None] * freqs[None, :]
    cos, sin = jnp.cos(ang), jnp.sin(ang)
    pad = A_HEAD_DIM - ROT_DIM
    one = jnp.ones((t, pad), _F32)
    zero = jnp.zeros((t, pad), _F32)
    zh = jnp.zeros((t, half), _F32)
    c = jnp.concatenate([cos, cos, one], axis=1)
    s1 = jnp.concatenate([zh, sin, zero], axis=1)
    s2 = jnp.concatenate([-sin, zh, zero], axis=1)
    return tuple(jnp.tile(jnp.tile(a, (1, 2)), (reps, 1)) for a in (c, s1, s2))


def kernel(x_prompt, x_sample, cache_k, cache_v, cache_conv, norm_mix_pre, norm_mix_post,
           norm_mlp_pre, norm_mlp_post, w_in, diff_lambda, diff_subln, gmlp_norm, gmlp_w_s,
           gmlp_bias, conv_w, conv_b, conv_ln_gain, conv_ln_bias, w_out, w_up, w_down):
    depth = w_in.shape[0]
    bp, tp, _ = x_prompt.shape
    bs, ts, _ = x_sample.shape
    past = cache_k.shape[2]
    cache_kt = jnp.transpose(cache_k, (0, 1, 3, 4, 2))
    cache_v4 = cache_v.reshape(depth, bs, past * A_HEADS, PAIR)
    tabs_p = _rope_tables(0, tp, 1)
    tabs_pt = tuple(a.T for a in tabs_p)
    tabs_s = _rope_tables(past, ts, FRONT_ROWS // ts)
    w_in_b, w_out_b = w_in.astype(_BF16), w_out.astype(_BF16)
    w_up_b, w_down_b = w_up.astype(_BF16), w_down.astype(_BF16)
    w_k_t = jnp.swapaxes(w_in_b[:, :, A_WIDTH:2 * A_WIDTH], 1, 2)

    def layer_weights(l, chunk):
        return dict(
            g_pre=norm_mix_pre[l][None], g_post=norm_mix_post[l][None],
            g_mlp_pre=norm_mlp_pre[l][None], g_mlp_post=norm_mlp_post[l][None],
            layer=l, w_in=w_in_b, w_k_t=w_k_t,
            lam_qk=diff_lambda[l], subln=diff_subln[l][None],
            gmlp_norm=gmlp_norm[l][None], w_s=gmlp_w_s[l][:, :chunk, :chunk],
            bias=jnp.repeat(gmlp_bias[l][:, :chunk].T, B_HEAD_DIM, axis=1),
            conv_w=conv_w[l], conv_b=conv_b[l][None], ln_g=conv_ln_gain[l][None],
            ln_b=conv_ln_bias[l][None], w_out=w_out_b, w_up=w_up_b, w_down=w_down_b)

    xp, xs = x_prompt, x_sample
    kv_all = None
    outs = [[] for _ in range(5)]
    for l in range(depth):
        lam_init = 0.8 - 0.6 * math.exp(-0.3 * l)

        lw = layer_weights(l, min(tp, GMLP_CHUNK))
        q, kt_all, v4_all, kbt, vb, bc, c_p = _front(xp, tabs_p, lw, layer=l, depth=depth,
                                                   tabs_t=tabs_pt, kv_all=kv_all)
        kv_all = (kt_all, v4_all)
        a = _attn_prompt(q, kbt, vb, lw['lam_qk'], lw['subln'], lam_init)
        xp = _back(xp.reshape(bp * tp, D_MODEL), a.reshape(bp * tp, A_WIDTH),
                   bc.reshape(bp * tp, A_WIDTH), lw).reshape(bp, tp, D_MODEL)

        lw = layer_weights(l, min(ts, GMLP_CHUNK))
        q, k_s, v_s, g_s, bc, c_s = _front(xs, tabs_s, lw, cache_conv=cache_conv[l])
        a = _attn_sample(q, k_s, v_s, cache_kt, cache_v4, l, lw['lam_qk'], lw['subln'], lam_init)
        xs = _back(xs.reshape(bs * ts, D_MODEL), a.reshape(bs * ts, A_WIDTH),
                   bc.reshape(bs * ts, A_WIDTH), lw).reshape(bs, ts, D_MODEL)

        for lst, val in zip(outs, (c_p, k_s, v_s, c_s, g_s)):
            lst.append(val)
    cp, ksl, vsl, csl, gsl = (jnp.stack(o) for o in outs)
    kt_all, v4_all = kv_all
    kp = jnp.transpose(kt_all.reshape(depth, bp, 2 * A_HEADS, A_HEAD_DIM, tp), (0, 1, 4, 2, 3))
    return (xp, xs, kp,
            v4_all.reshape(depth, bp, tp, A_HEADS, 2 * A_HEAD_DIM),
            cp,
            ksl.reshape(depth, bs, ts, 2 * A_HEADS, A_HEAD_DIM),
            vsl.reshape(depth, bs, ts, A_HEADS, 2 * A_HEAD_DIM),
            csl,
            gsl.reshape(depth, bs, ts, B_HEADS, B_HEAD_DIM))
```

```python
import functools
import math

import jax
import jax.numpy as jnp
from jax import lax
from jax.experimental import pallas as pl
from jax.experimental.pallas import tpu as pltpu

D_MODEL = 1024
A_HEADS = 4
A_HEAD_DIM = 64
A_WIDTH = A_HEADS * 2 * A_HEAD_DIM
PAIR = 2 * A_HEAD_DIM
ROT_DIM = A_HEAD_DIM // 4
ROPE_THETA = 500000.0
B_HEADS = 4
B_HEAD_DIM = 64
B_WIDTH = B_HEADS * B_HEAD_DIM
GMLP_CHUNK = 128
C_WIDTH = 256
CONV_WIDTH = 31
HIST = CONV_WIDTH - 1
SUBLANES = 8
HIST_PAD = 32
D_IN = 3 * A_WIDTH + 2 * B_WIDTH + 2 * C_WIDTH
D_FF = 4 * D_MODEL
EPS = 1e-6
NEG_INF = -1e30
LOG2_E = math.log2(math.e)
CHUNK = 64
CHUNK_SHIFT = CHUNK.bit_length() - 1

VMEM_LIMIT_BYTES = 56 * 1024 * 1024
ATTN_TILE = 1024
FRONT_ROWS = ATTN_TILE
ATTN_ROWS = 128
CACHE_TILE = 2048
BACK_ROWS = 512
FF_CHUNK = 1024

_BF16 = jnp.bfloat16
_F32 = jnp.float32
_NT = (((1,), (1,)), ((), ()))


def _rms(x, g):
    return x * lax.rsqrt(jnp.mean(x * x, axis=-1, keepdims=True) + EPS) * g


def _gelu_tanh(x):
    c = math.sqrt(2.0 / math.pi)
    return 0.5 * x * (1.0 + jnp.tanh(c * (x + 0.044715 * (x * x * x))))


def _sigmoid(x):
    return 1.0 / (1.0 + jnp.exp(-x))


def _rope_pairs(z, c, s1, s2, axis):
    outs = []
    for j in range(A_WIDTH // PAIR):
        zj = z[:, j * PAIR:(j + 1) * PAIR] if axis == 1 else z[j * PAIR:(j + 1) * PAIR, :]
        up = pltpu.roll(zj, ROT_DIM // 2, axis=axis)
        dn = pltpu.roll(zj, PAIR - ROT_DIM // 2, axis=axis)
        outs.append(zj * c + up * s1 + dn * s2)
    return jnp.concatenate(outs, axis=axis)


_FRONT_COMMON = ('x', 'c', 's1', 's2', 'gpre', 'win', 'gn', 'ws', 'bias', 'cw', 'cb', 'lg', 'lb')


def _front_kernel(*refs, bb, tt, chunk, prompt, n_alias):
    if prompt:
        names = _FRONT_COMMON + ('ct', 's1t', 's2t', 'wkt') + ('alias',) * n_alias + (
            'q', 'kt', 'v4', 'kbt', 'vb', 'bc', 'nc', 'buf', 'xs')
    else:
        names = _FRONT_COMMON + ('cache', 'q', 'k', 'v', 'vn', 'bc', 'nc', 'buf', 'xs')
    r = dict(zip(names, refs))
    ti = pl.program_id(1)
    m = bb * tt

    x = r['x'][...].reshape(m, D_MODEL)
    h = _rms(x, r['gpre'][...]).astype(_BF16)
    win_ref = r['win']

    def proj(lo, hi):
        return jnp.dot(h, win_ref[:, lo:hi], preferred_element_type=_F32)

    o = 3 * A_WIDTH + 2 * B_WIDTH
    glu = proj(o, o + C_WIDTH) * _sigmoid(proj(o + C_WIDTH, o + 2 * C_WIDTH))
    glu = glu.reshape(bb, tt, C_WIDTH)
    buf_ref, xs_ref = r['buf'], r['xs']
    lo = HIST_PAD - HIST
    if prompt:
        @pl.when(ti == 0)
        def _():
            buf_ref[:, 0:HIST_PAD, :] = jnp.zeros((bb, HIST_PAD, C_WIDTH), _F32)

        @pl.when(ti > 0)
        def _():
            buf_ref[:, 0:HIST_PAD, :] = buf_ref[:, tt:tt + HIST_PAD, :]
    else:
        buf_ref[:, 0:lo, :] = jnp.zeros((bb, lo, C_WIDTH), _F32)
        buf_ref[:, lo:HIST_PAD, :] = r['cache'][...]
    buf_ref[:, HIST_PAD:HIST_PAD + tt, :] = glu
    for p in range(SUBLANES):
        n = tt + SUBLANES * ((CONV_WIDTH - 1 - p) // SUBLANES)
        xs_ref[p, :, 0:n, :] = buf_ref[:, lo + p:lo + p + n, :]
    cw = r['cw'][...]
    y = jnp.zeros((bb, tt, C_WIDTH), _F32) + r['cb'][...]
    for j in range(CONV_WIDTH):
        a, p = divmod(j, SUBLANES)
        y = y + cw[j:j + 1, :] * xs_ref[p, :, SUBLANES * a:SUBLANES * a + tt, :]
    mu = jnp.mean(y, axis=-1, keepdims=True)
    yc = y - mu
    var = jnp.mean(yc * yc, axis=-1, keepdims=True)
    yn = yc * lax.rsqrt(var + EPS) * r['lg'][...] + r['lb'][...]
    c_out = (yn * _sigmoid(yn)).reshape(m, C_WIDTH)

    c, s1, s2 = r['c'][...], r['s1'][...], r['s2'][...]
    q = _rope_pairs(proj(0, A_WIDTH), c, s1, s2, 1) * (A_HEAD_DIM ** -0.5 * LOG2_E)
    r['q'][...] = q.astype(_BF16).reshape(bb, tt, A_WIDTH)
    v = proj(2 * A_WIDTH, 3 * A_WIDTH)
    if prompt:
        kt = lax.dot_general(r['wkt'][...], h, _NT, preferred_element_type=_F32)
        kt = _rope_pairs(kt, r['ct'][...], r['s1t'][...], r['s2t'][...], 0)
        r['kt'][0] = kt
        r['kbt'][0, 0] = kt.astype(_BF16)
        for j in range(A_HEADS):
            r['v4'][0, pl.ds(j, tt, stride=A_HEADS), :] = v[:, j * PAIR:(j + 1) * PAIR]
        r['vb'][...] = v.astype(_BF16).reshape(bb, tt, A_WIDTH)
    else:
        k = _rope_pairs(proj(A_WIDTH, 2 * A_WIDTH), c, s1, s2, 1)
        r['k'][...] = k.reshape(bb, tt, A_WIDTH)
        r['v'][...] = v.reshape(bb, tt, A_WIDTH)

    o = 3 * A_WIDTH
    u = _gelu_tanh(proj(o, o + B_WIDTH))
    vn = _rms(_gelu_tanh(proj(o + B_WIDTH, o + 2 * B_WIDTH)), r['gn'][...])
    if not prompt:
        r['vn'][...] = vn.reshape(bb, tt, B_WIDTH)
    vnb = vn.astype(_BF16)
    row = lax.broadcasted_iota(jnp.int32, (chunk, chunk), 0)
    col = lax.broadcasted_iota(jnp.int32, (chunk, chunk), 1)
    tri = row >= col
    ws = jnp.concatenate(
        [jnp.where(tri, r['ws'][hd], 0.0).astype(_BF16) for hd in range(B_HEADS)], axis=0)
    lane = lax.broadcasted_iota(jnp.int32, (chunk, B_WIDTH), 1)
    bias = r['bias'][...]
    mixed_parts = []
    for r0 in range(0, m, chunk):
        mm = jnp.dot(ws, vnb[r0:r0 + chunk], preferred_element_type=_F32)
        mixed = mm[0:chunk]
        for hd in range(1, B_HEADS):
            mixed = jnp.where(lane >= hd * B_HEAD_DIM, mm[hd * chunk:(hd + 1) * chunk], mixed)
        mixed_parts.append(mixed + bias)
    b_out = u * jnp.concatenate(mixed_parts, axis=0)

    r['bc'][...] = jnp.concatenate([b_out, c_out], axis=1).astype(_BF16).reshape(
        bb, tt, B_WIDTH + C_WIDTH)

    @pl.when(ti == pl.num_programs(1) - 1)
    def _():
        r['nc'][...] = buf_ref[:, tt + lo:tt + HIST_PAD, :]


def _sample_batches(bsz, t):
    bb = min(bsz, max(1, FRONT_ROWS // t))
    assert bsz % bb == 0, (bsz, bb)
    return bb


def _front(x, tabs, lw, *, layer=0, depth=1, tabs_t=None, kv_all=None, cache_conv=None):
    bsz, t, _ = x.shape
    prompt = tabs_t is not None
    if prompt:
        bb, tt = 1, FRONT_ROWS
    else:
        bb, tt = _sample_batches(bsz, t), t
    chunk = min(t, GMLP_CHUNK)
    m = bb * tt
    nt = t // tt
    full = lambda shape: pl.BlockSpec(shape, lambda b, i: (0,) * len(shape))
    wmat = lambda shape: pl.BlockSpec((None,) + shape, lambda b, i: (lw['layer'], 0, 0))
    act = lambda w: pl.BlockSpec((bb, tt, w), lambda b, i: (b, i, 0))
    tab = pl.BlockSpec((m, PAIR), lambda b, i: (i, 0))
    hist = pl.BlockSpec((bb, HIST, C_WIDTH), lambda b, i: (b, 0, 0))
    in_specs = [act(D_MODEL), tab, tab, tab, full((1, D_MODEL)), wmat((D_MODEL, D_IN)),
                full((1, B_WIDTH)), full((B_HEADS, chunk, chunk)), full((chunk, B_WIDTH)),
                full((CONV_WIDTH, C_WIDTH)), full((1, C_WIDTH)), full((1, C_WIDTH)),
                full((1, C_WIDTH))]
    args = [x, *tabs, lw['g_pre'], lw['w_in'], lw['gmlp_norm'], lw['w_s'], lw['bias'],
            lw['conv_w'], lw['conv_b'], lw['ln_g'], lw['ln_b']]
    sds = jax.ShapeDtypeStruct
    aliases = {}
    if prompt:
        tab_t = pl.BlockSpec((PAIR, tt), lambda b, i: (0, i))
        in_specs += [tab_t, tab_t, tab_t, wmat((A_WIDTH, D_MODEL))]
        args += [*tabs_t, lw['w_k_t']]
        if kv_all is not None:
            aliases = {len(args): 1, len(args) + 1: 2}
            in_specs += [pl.BlockSpec(memory_space=pl.ANY)] * 2
            args += list(kv_all)
        out_shape = [sds((bsz, t, A_WIDTH), _BF16),
                     sds((depth, bsz, A_WIDTH, t), _F32),
                     sds((depth, bsz, t * A_HEADS, PAIR), _F32),
                     sds((bsz, nt, A_WIDTH, tt), _BF16),
                     sds((bsz, t, A_WIDTH), _BF16), sds((bsz, t, A_WIDTH), _BF16),
                     sds((bsz, HIST, C_WIDTH), _F32)]
        out_specs = [act(A_WIDTH),
                     pl.BlockSpec((None, 1, A_WIDTH, tt), lambda b, i: (layer, b, 0, i)),
                     pl.BlockSpec((None, 1, tt * A_HEADS, PAIR), lambda b, i: (layer, b, i, 0)),
                     pl.BlockSpec((1, 1, A_WIDTH, tt), lambda b, i: (b, i, 0, 0)),
                     act(A_WIDTH), act(A_WIDTH), hist]
    else:
        in_specs.append(hist)
        args.append(cache_conv)
        out_shape = [sds((bsz, t, A_WIDTH), _BF16), sds((bsz, t, A_WIDTH), _F32),
                     sds((bsz, t, A_WIDTH), _F32), sds((bsz, t, B_WIDTH), _F32),
                     sds((bsz, t, A_WIDTH), _BF16), sds((bsz, HIST, C_WIDTH), _F32)]
        out_specs = [act(A_WIDTH)] * 3 + [act(B_WIDTH), act(A_WIDTH), hist]
    return pl.pallas_call(
        functools.partial(_front_kernel, bb=bb, tt=tt, chunk=chunk, prompt=prompt,
                          n_alias=len(aliases)),
        grid=(bsz // bb, nt),
        in_specs=in_specs,
        out_specs=out_specs,
        out_shape=out_shape,
        input_output_aliases=aliases,
        scratch_shapes=[pltpu.VMEM((bb, HIST_PAD + tt, C_WIDTH), _F32),
                        pltpu.VMEM((SUBLANES, bb, tt + HIST_PAD - SUBLANES, C_WIDTH), _F32)],
        compiler_params=pltpu.CompilerParams(
            dimension_semantics=("arbitrary", "arbitrary"), vmem_limit_bytes=VMEM_LIMIT_BYTES),
        name="front_prompt" if prompt else "front_sample",
    )(*args)


def _diff_lambda(lam_ref, lam_init):
    lq = lam_ref[...]
    a = jnp.sum(lq[0:1] * lq[1:2], axis=-1, keepdims=True)
    b = jnp.sum(lq[2:3] * lq[3:4], axis=-1, keepdims=True)
    return jnp.exp(a) - jnp.exp(b) + lam_init


def _stack_query(q, n):
    lane = lax.broadcasted_iota(jnp.int32, (n, PAIR), 1)
    zero = jnp.zeros_like(q)
    return jnp.concatenate([jnp.where(lane < A_HEAD_DIM, q, zero),
                            jnp.where(lane >= A_HEAD_DIM, q, zero)], axis=0)


def _finish_head(acc, n, lam, subln, lam_init):
    o = acc[:, 0:PAIR] / acc[:, PAIR:2 * PAIR]
    d = o[0:n] - lam * o[n:2 * n]
    return _rms(d, subln) * (1.0 - lam_init)


def _with_ones(v):
    return jnp.concatenate([v, jnp.ones_like(v)], axis=1)


def _online_rows(blocks, vt, m_prev, acc_prev):
    m_cur = jnp.max(functools.reduce(jnp.maximum, blocks), axis=-1, keepdims=True)
    if m_prev is None:
        m_new = jnp.broadcast_to(m_cur, blocks[0].shape)
    else:
        m_new = jnp.maximum(m_prev, m_cur)
    p = jnp.concatenate([jnp.exp2((blk - m_new).astype(_BF16)) for blk in blocks], axis=1)
    acc_new = jnp.dot(p, vt, preferred_element_type=_F32)
    if m_prev is not None:
        alpha = jnp.exp2(m_prev - m_new)
        acc_new = jnp.concatenate([alpha, alpha], axis=1) * acc_prev + acc_new
    return m_new, acc_new


def _attn_prompt_kernel(lam_ref, sub_ref, q_ref, k_ref, v_ref, o_ref, qm_ref, va_ref, m_ref,
                        acc_ref, *, tq, rows, lam_init):
    nt = k_ref.shape[1]
    va_ref[...] = _with_ones(v_ref[0])
    chunks = range(0, 2 * tq, rows)
    steps = [(qi, ki) for qi in range(nt) for ki in [qi] + list(range(qi))]

    def key_counts(qi, ki):
        return [(r & (tq - 1)) + rows if ki == qi else tq for r in chunks]

    def score_matmuls(qi, ki):
        if ki == qi:
            qm_ref[qi] = _stack_query(q_ref[0, qi * tq:(qi + 1) * tq, :], tq)
        kt = k_ref[0, ki]
        return [jnp.dot(qm_ref[qi, r:r + rows], kt[:, 0:n], preferred_element_type=_F32)
                for r, n in zip(chunks, key_counts(qi, ki))]

    def softmax_chains(qi, ki, scores):
        diagonal = ki == qi
        vt = va_ref[ki * tq:(ki + 1) * tq, :]
        for r, n, s in zip(chunks, key_counts(qi, ki), scores):
            sl = slice(r, r + rows)
            q0 = r & (tq - 1)
            blocks = []
            for c in range(0, n, PAIR):
                blk = s[:, c:c + PAIR]
                if diagonal and c + PAIR > q0:
                    row = lax.broadcasted_iota(jnp.int32, (rows, PAIR), 0) + q0
                    col = lax.broadcasted_iota(jnp.int32, (rows, PAIR), 1) + c
                    blk = jnp.where((col >> CHUNK_SHIFT) <= (row >> CHUNK_SHIFT), blk, NEG_INF)
                blocks.append(blk)
            prev = (None, None) if diagonal else (m_ref[qi, sl], acc_ref[qi, sl])
            m_ref[qi, sl], acc_ref[qi, sl] = _online_rows(blocks, vt[0:n], *prev)

    lam = _diff_lambda(lam_ref, lam_init)
    scores = score_matmuls(*steps[0])
    for i, (qi, ki) in enumerate(steps):
        cur = scores
        if i + 1 < len(steps):
            scores = score_matmuls(*steps[i + 1])
        softmax_chains(qi, ki, cur)
        if i + 1 == len(steps) or steps[i + 1][0] != qi:
            o_ref[0, qi * tq:(qi + 1) * tq, :] = _finish_head(
                acc_ref[qi], tq, lam, sub_ref[...], lam_init).astype(_BF16)


def _attn_prompt(q, kbt, vb, lam_qk, subln, lam_init):
    bsz, t, _ = q.shape
    tq = ATTN_TILE
    nt = t // tq
    spec = pl.BlockSpec((1, t, PAIR), lambda b, j: (b, 0, j))
    kspec = pl.BlockSpec((1, nt, PAIR, tq), lambda b, j: (b, 0, j, 0))
    return pl.pallas_call(
        functools.partial(_attn_prompt_kernel, tq=tq, rows=ATTN_ROWS, lam_init=lam_init),
        grid=(bsz, A_HEADS),
        in_specs=[pl.BlockSpec((4, A_HEAD_DIM), lambda b, j: (0, 0)),
                  pl.BlockSpec((1, PAIR), lambda b, j: (0, 0)),
                  spec, kspec, spec],
        out_specs=spec,
        out_shape=jax.ShapeDtypeStruct((bsz, t, A_WIDTH), _BF16),
        scratch_shapes=[pltpu.VMEM((nt, 2 * tq, PAIR), _BF16), pltpu.VMEM((t, 2 * PAIR), _BF16),
                        pltpu.VMEM((nt, 2 * tq, PAIR), _F32),
                        pltpu.VMEM((nt, 2 * tq, 2 * PAIR), _F32)],
        compiler_params=pltpu.CompilerParams(
            dimension_semantics=("arbitrary",) * 2, vmem_limit_bytes=VMEM_LIMIT_BYTES),
        name="attn_prompt",
    )(lam_qk, subln, q, kbt, vb)


def _attn_sample_kernel(lam_ref, sub_ref, q_ref, kn_ref, vn_ref, kc_ref, vc_ref, o_ref,
                        qm_ref, m_ref, acc_ref, *, t, lam_init):
    ki = pl.program_id(1)
    tk = kc_ref.shape[2]

    @pl.when(ki == 0)
    def _():
        q = q_ref[0]
        pad = jnp.zeros((PAIR - t, PAIR), _BF16)
        col = lax.broadcasted_iota(jnp.int32, (2 * t, PAIR), 1)
        for j in range(A_HEADS):
            sl = slice(j * PAIR, (j + 1) * PAIR)
            qm_ref[j] = _stack_query(q[:, sl], t)
            kj = jnp.concatenate([kn_ref[0, :, sl].astype(_BF16), pad], axis=0)
            vj = jnp.concatenate([vn_ref[0, :, sl].astype(_BF16), pad], axis=0)
            s = lax.dot_general(qm_ref[j], kj, _NT, preferred_element_type=_F32)
            s = jnp.where(col < t, s, NEG_INF)
            m_ref[j], acc_ref[j] = _online_rows([s], _with_ones(vj), None, None)

    for j in range(A_HEADS):
        kj = kc_ref[2 * j:2 * j + 2].reshape(PAIR, tk).astype(_BF16)
        vj = vc_ref[pl.ds(j, tk, stride=A_HEADS), :].astype(_BF16)
        s = jnp.dot(qm_ref[j], kj, preferred_element_type=_F32)
        blocks = [s[:, c:c + PAIR] for c in range(0, tk, PAIR)]
        m_ref[j], acc_ref[j] = _online_rows(blocks, _with_ones(vj), m_ref[j], acc_ref[j])

    @pl.when(ki == pl.num_programs(1) - 1)
    def _():
        lam = _diff_lambda(lam_ref, lam_init)
        for j in range(A_HEADS):
            o_ref[0, :, j * PAIR:(j + 1) * PAIR] = _finish_head(
                acc_ref[j], t, lam, sub_ref[...], lam_init).astype(_BF16)


def _attn_sample(q, k_new, v_new, cache_kt, cache_v4, layer, lam_qk, subln, lam_init):
    bsz, t, _ = q.shape
    past = cache_kt.shape[-1]
    tk = CACHE_TILE
    new = pl.BlockSpec((1, t, A_WIDTH), lambda b, ki: (b, 0, 0))
    kspec = pl.BlockSpec((None, None, 2 * A_HEADS, A_HEAD_DIM, tk),
                         lambda b, ki: (layer, b, 0, 0, ki))
    vspec = pl.BlockSpec((None, None, tk * A_HEADS, PAIR), lambda b, ki: (layer, b, ki, 0))
    return pl.pallas_call(
        functools.partial(_attn_sample_kernel, t=t, lam_init=lam_init),
        grid=(bsz, past // tk),
        in_specs=[pl.BlockSpec((4, A_HEAD_DIM), lambda b, ki: (0, 0)),
                  pl.BlockSpec((1, PAIR), lambda b, ki: (0, 0)),
                  new, new, new, kspec, vspec],
        out_specs=new,
        out_shape=jax.ShapeDtypeStruct((bsz, t, A_WIDTH), _BF16),
        scratch_shapes=[pltpu.VMEM((A_HEADS, 2 * t, PAIR), _BF16),
                        pltpu.VMEM((A_HEADS, 2 * t, PAIR), _F32),
                        pltpu.VMEM((A_HEADS, 2 * t, 2 * PAIR), _F32)],
        compiler_params=pltpu.CompilerParams(
            dimension_semantics=("arbitrary", "arbitrary"), vmem_limit_bytes=VMEM_LIMIT_BYTES),
        name="attn_sample",
    )(lam_qk, subln, q, k_new, v_new, cache_kt, cache_v4)


def _back_kernel(x_ref, a_ref, bc_ref, wout_ref, gpost_ref, gpre_ref, wup_ref, wdown_ref,
                 gmlp_ref, o_ref):
    mix = jnp.dot(a_ref[...], wout_ref[0:A_WIDTH, :], preferred_element_type=_F32)
    mix = mix + jnp.dot(bc_ref[...], wout_ref[A_WIDTH:, :], preferred_element_type=_F32)
    x1 = x_ref[...] + _rms(mix, gpost_ref[...])
    h = _rms(x1, gpre_ref[...]).astype(_BF16)
    f = jnp.zeros(x1.shape, _F32)
    for c0 in range(0, D_FF, FF_CHUNK):
        up = jnp.dot(h, wup_ref[:, c0:c0 + FF_CHUNK], preferred_element_type=_F32)
        up = jnp.maximum(up, 0.0)
        f = f + jnp.dot((up * up).astype(_BF16), wdown_ref[c0:c0 + FF_CHUNK, :],
                        preferred_element_type=_F32)
    o_ref[...] = x1 + _rms(f, gmlp_ref[...])


def _back(x, a, bc, lw):
    m = x.shape[0]
    tm = BACK_ROWS
    full = lambda shape: pl.BlockSpec(shape, lambda i: (0, 0))
    wmat = lambda shape: pl.BlockSpec((None,) + shape, lambda i: (lw['layer'], 0, 0))
    rows = lambda w: pl.BlockSpec((tm, w), lambda i: (i, 0))
    return pl.pallas_call(
        _back_kernel,
        grid=(m // tm,),
        in_specs=[rows(D_MODEL), rows(A_WIDTH), rows(A_WIDTH), wmat((D_MODEL, D_MODEL)),
                  full((1, D_MODEL)), full((1, D_MODEL)), wmat((D_MODEL, D_FF)),
                  wmat((D_FF, D_MODEL)), full((1, D_MODEL))],
        out_specs=rows(D_MODEL),
        out_shape=jax.ShapeDtypeStruct((m, D_MODEL), _F32),
        compiler_params=pltpu.CompilerParams(
            dimension_semantics=("arbitrary",), vmem_limit_bytes=VMEM_LIMIT_BYTES),
        name="back",
    )(x, a, bc, lw['w_out'], lw['g_post'], lw['g_mlp_pre'], lw['w_up'], lw['w_down'],
      lw['g_mlp_post'])


def _rope_tables(pos0, t, reps):
    half = ROT_DIM // 2
    freqs = ROPE_THETA ** (-jnp.arange(0, ROT_DIM, 2, dtype=_F32) / ROT_DIM)
    pos = (pos0 + jnp.arange(t)).astype(_F32)
    ang = pos[:, None] * freqs[None, :]
    cos, sin = jnp.cos(ang), jnp.sin(ang)
    pad = A_HEAD_DIM - ROT_DIM
    one = jnp.ones((t, pad), _F32)
    zero = jnp.zeros((t, pad), _F32)
    zh = jnp.zeros((t, half), _F32)
    c = jnp.concatenate([cos, cos, one], axis=1)
    s1 = jnp.concatenate([zh, sin, zero], axis=1)
    s2 = jnp.concatenate([-sin, zh, zero], axis=1)
    return tuple(jnp.tile(jnp.tile(a, (1, 2)), (reps, 1)) for a in (c, s1, s2))


def kernel(x_prompt, x_sample, cache_k, cache_v, cache_conv, norm_mix_pre, norm_mix_post,
           norm_mlp_pre, norm_mlp_post, w_in, diff_lambda, diff_subln, gmlp_norm, gmlp_w_s,
           gmlp_bias, conv_w, conv_b, conv_ln_gain, conv_ln_bias, w_out, w_up, w_down):
    depth = w_in.shape[0]
    bp, tp, _ = x_prompt.shape
    bs, ts, _ = x_sample.shape
    past = cache_k.shape[2]
    cache_kt = jnp.transpose(cache_k, (0, 1, 3, 4, 2))
    cache_v4 = cache_v.reshape(depth, bs, past * A_HEADS, PAIR)
    tabs_p = _rope_tables(0, tp, 1)
    tabs_pt = tuple(a.T for a in tabs_p)
    tabs_s = _rope_tables(past, ts, _sample_batches(bs, ts))
    w_in_b, w_out_b = w_in.astype(_BF16), w_out.astype(_BF16)
    w_up_b, w_down_b = w_up.astype(_BF16), w_down.astype(_BF16)
    w_k_t = jnp.swapaxes(w_in_b[:, :, A_WIDTH:2 * A_WIDTH], 1, 2)

    def layer_weights(l, chunk):
        return dict(
            g_pre=norm_mix_pre[l][None], g_post=norm_mix_post[l][None],
            g_mlp_pre=norm_mlp_pre[l][None], g_mlp_post=norm_mlp_post[l][None],
            layer=l, w_in=w_in_b, w_k_t=w_k_t,
            lam_qk=diff_lambda[l], subln=diff_subln[l][None],
            gmlp_norm=gmlp_norm[l][None], w_s=gmlp_w_s[l][:, :chunk, :chunk],
            bias=jnp.repeat(gmlp_bias[l][:, :chunk].T, B_HEAD_DIM, axis=1),
            conv_w=conv_w[l], conv_b=conv_b[l][None], ln_g=conv_ln_gain[l][None],
            ln_b=conv_ln_bias[l][None], w_out=w_out_b, w_up=w_up_b, w_down=w_down_b)

    xp, xs = x_prompt, x_sample
    kv_all = None
    outs = [[] for _ in range(5)]
    for l in range(depth):
        lam_init = 0.8 - 0.6 * math.exp(-0.3 * l)

        lw = layer_weights(l, min(tp, GMLP_CHUNK))
        q, kt_all, v4_all, kbt, vb, bc, c_p = _front(xp, tabs_p, lw, layer=l, depth=depth,
                                                   tabs_t=tabs_pt, kv_all=kv_all)
        kv_all = (kt_all, v4_all)
        a = _attn_prompt(q, kbt, vb, lw['lam_qk'], lw['subln'], lam_init)
        xp = _back(xp.reshape(bp * tp, D_MODEL), a.reshape(bp * tp, A_WIDTH),
                   bc.reshape(bp * tp, A_WIDTH), lw).reshape(bp, tp, D_MODEL)

        lw = layer_weights(l, min(ts, GMLP_CHUNK))
        q, k_s, v_s, g_s, bc, c_s = _front(xs, tabs_s, lw, cache_conv=cache_conv[l])
        a = _attn_sample(q, k_s, v_s, cache_kt, cache_v4, l, lw['lam_qk'], lw['subln'], lam_init)
        xs = _back(xs.reshape(bs * ts, D_MODEL), a.reshape(bs * ts, A_WIDTH),
                   bc.reshape(bs * ts, A_WIDTH), lw).reshape(bs, ts, D_MODEL)

        for lst, val in zip(outs, (c_p, k_s, v_s, c_s, g_s)):
            lst.append(val)
    cp, ksl, vsl, csl, gsl = (jnp.stack(o) for o in outs)
    kt_all, v4_all = kv_all
    kp = jnp.transpose(kt_all.reshape(depth, bp, 2 * A_HEADS, A_HEAD_DIM, tp), (0, 1, 4, 2, 3))
    return (xp, xs, kp,
            v4_all.reshape(depth, bp, tp, A_HEADS, 2 * A_HEAD_DIM),
            cp,
            ksl.reshape(depth, bs, ts, 2 * A_HEADS, A_HEAD_DIM),
            vsl.reshape(depth, bs, ts, A_HEADS, 2 * A_HEAD_DIM),
            csl,
            gsl.reshape(depth, bs, ts, B_HEADS, B_HEAD_DIM))
```
